```python
import jax, jax.numpy as jnp
from jax import lax
import numpy as np

D_MODEL = 1024
BATCH = 4
SEQ = 8192
DEPTH = 1

GLA_HEADS = 4
GLA_DK = 128
GLA_DV = 256
GLA_KEY_W = GLA_HEADS * GLA_DK
GLA_VAL_W = GLA_HEADS * GLA_DV
GLA_GATE_RANK = 16
GLA_TAU = 16.0
GLA_CHUNK = 64
SWA_HEADS = 16
SWA_KV_HEADS = 2
SWA_GROUP = SWA_HEADS // SWA_KV_HEADS
SWA_DH = 64
SWA_W = SWA_HEADS * SWA_DH
SWA_KV_W = SWA_KV_HEADS * SWA_DH
WINDOW = 128
N_GROUPS = 4
EXPERTS_PER_GROUP = 4
N_EXPERTS = N_GROUPS * EXPERTS_PER_GROUP
TOP_K = 2
D_EXPERT = 256
N_MOD = 6
EPS = 1e-6
IN_WIDTHS = (GLA_KEY_W, GLA_KEY_W, GLA_VAL_W, GLA_VAL_W, GLA_GATE_RANK,
             SWA_W, SWA_KV_W, SWA_KV_W, D_MODEL, D_MODEL)
IN_SPLITS = tuple(int(s) for s in np.cumsum(IN_WIDTHS)[:-1])
IN_TOTAL = int(sum(IN_WIDTHS))

kernel_name = "hybrid_gla_swa_sink_hmoe_adaln"


def rms_norm(x, g):
    x32 = x.astype(jnp.float32)
    y = x32 * lax.rsqrt(jnp.mean(x32 * x32, axis=-1, keepdims=True) + EPS)
    return (y * g.astype(jnp.float32)).astype(x.dtype)


def gla_mixer(q, k, v, r, z_gate, w_gk2, b_gk, gla_norm_g):
    B, S, _ = q.shape
    N = S // GLA_CHUNK
    dt = q.dtype
    f32 = jnp.float32
    shp = (B, N, GLA_CHUNK, GLA_HEADS)
    q = q.astype(f32).reshape(*shp, GLA_DK) * (GLA_DK ** -0.5)
    k = k.astype(f32).reshape(*shp, GLA_DK)
    v = v.astype(f32).reshape(*shp, GLA_DV)
    log_a = jax.nn.log_sigmoid((z_gate @ w_gk2 + b_gk).astype(f32)) / GLA_TAU
    log_a = log_a.reshape(*shp, GLA_DK)
    b = jnp.cumsum(log_a, axis=2)
    b_last = b[:, :, -1]
    q_dec = q * jnp.exp(b)
    k_inv = k * jnp.exp(-b)
    causal = jnp.tril(jnp.ones((GLA_CHUNK, GLA_CHUNK), dtype=bool))
    att = jnp.einsum('bnihk,bnjhk->bnhij', q_dec, k_inv)
    att = jnp.where(causal, att, 0.0)
    o_intra = jnp.einsum('bnhij,bnjhv->bnihv', att, v)
    k_to_end = k * jnp.exp(b_last[:, :, None] - b)
    upd = jnp.einsum('bnjhk,bnjhv->bnhkv', k_to_end, v)
    decay = jnp.exp(b_last)

    def step(state, inp):
        dec, u = inp
        return dec[..., None] * state + u, state

    s0 = jnp.zeros((B, GLA_HEADS, GLA_DK, GLA_DV), f32)
    _, s_prev = lax.scan(step, s0, (jnp.moveaxis(decay, 1, 0), jnp.moveaxis(upd, 1, 0)))
    s_prev = jnp.moveaxis(s_prev, 0, 1)
    o_inter = jnp.einsum('bnihk,bnhkv->bnihv', q_dec, s_prev)
    o = (o_intra + o_inter).reshape(B, S, GLA_HEADS, GLA_DV)
    o = o * lax.rsqrt(jnp.mean(o * o, axis=-1, keepdims=True) + EPS) * gla_norm_g.astype(f32)
    o = o.reshape(B, S, GLA_VAL_W) * jax.nn.silu(r.astype(f32))
    return o.astype(dt)


def swa_mixer(q, k, v, sink):
    B, S, _ = q.shape
    nb = S // WINDOW
    dt = q.dtype
    f32 = jnp.float32
    q = q.reshape(B, nb, WINDOW, SWA_KV_HEADS, SWA_GROUP, SWA_DH)
    k = k.reshape(B, nb, WINDOW, SWA_KV_HEADS, SWA_DH)
    v = v.reshape(B, nb, WINDOW, SWA_KV_HEADS, SWA_DH)

    def with_prev(t):
        prev = jnp.concatenate([jnp.zeros_like(t[:, :1]), t[:, :-1]], axis=1)
        return jnp.concatenate([prev, t], axis=2)

    kk, vv = with_prev(k), with_prev(v)
    s = jnp.einsum('bnqhgd,bnkhd->bnhgqk', q, kk).astype(f32) * (SWA_DH ** -0.5)
    qi = jnp.arange(WINDOW)[:, None] + WINDOW
    kj = jnp.arange(2 * WINDOW)[None, :]
    rel = qi - kj
    band = (rel >= 0) & (rel < WINDOW)
    blk = jnp.arange(nb)[:, None, None]
    valid = band[None] & ((blk > 0) | (kj[None] >= WINDOW))
    s = jnp.where(valid[None, :, None, None], s, -jnp.inf)
    sink_logit = jnp.broadcast_to(
        sink.astype(f32).reshape(1, 1, SWA_KV_HEADS, SWA_GROUP, 1, 1), s.shape[:-1] + (1,))
    p = jax.nn.softmax(jnp.concatenate([s, sink_logit], axis=-1), axis=-1)[..., :-1]
    o = jnp.einsum('bnhgqk,bnkhd->bnqhgd', p.astype(dt), vv)
    return o.reshape(B, S, SWA_W)


def mixer(h, w_in, w_gk2, b_gk, gla_norm_g, sink, w_o):
    proj = h @ w_in
    qa, ka, va, ra, za, qb, kb, vb, ga, gb = jnp.split(proj, IN_SPLITS, axis=-1)
    oa = gla_mixer(qa, ka, va, ra, za, w_gk2, b_gk, gla_norm_g)
    ob = swa_mixer(qb, kb, vb, sink)
    merged = jax.nn.sigmoid(ga) * oa + jax.nn.sigmoid(gb) * ob
    return merged @ w_o


def hier_moe(h, w_group, b_group, w_router, b_router, w_gate, w_up, w_down):
    B, S, D = h.shape
    f32 = jnp.float32
    t = h.reshape(B * S, D)
    g_logits = (t @ w_group + b_group).astype(f32)
    g_prob = jax.nn.softmax(g_logits, axis=-1)
    g_idx = jnp.argmax(g_logits, axis=-1)
    g_w = jnp.take_along_axis(g_prob, g_idx[:, None], axis=1)
    e_logits = (t @ w_router + b_router).astype(f32).reshape(-1, N_GROUPS, EXPERTS_PER_GROUP)
    e_in = jnp.take_along_axis(e_logits, g_idx[:, None, None], axis=1)[:, 0]
    top_v, top_i = lax.top_k(e_in, TOP_K)
    top_w = jax.nn.softmax(top_v, axis=-1) * g_w
    e_idx = g_idx[:, None] * EXPERTS_PER_GROUP + top_i
    combine = jnp.einsum('tk,tke->te', top_w, jax.nn.one_hot(e_idx, N_EXPERTS, dtype=f32))
    hid = jax.nn.silu(jnp.einsum('td,edf->tef', t, w_gate)) * jnp.einsum('td,edf->tef', t, w_up)
    hid = hid * combine.astype(hid.dtype)[:, :, None]
    y = jnp.einsum('tef,efd->td', hid, w_down)
    return y.reshape(B, S, D)


def setup_inputs(seed: int = 0) -> dict:
    key = jax.random.key(seed)
    ks = jax.random.split(key, 20)
    f32 = jnp.float32
    L = DEPTH

    def nrm(k, shape, scale):
        return jax.random.normal(k, shape, f32) * scale

    return {
        "x": nrm(ks[0], (BATCH, SEQ, D_MODEL), 1.0),
        "c": nrm(ks[1], (BATCH, D_MODEL), 1.0),
        "w_ada": nrm(ks[2], (L, D_MODEL, N_MOD * D_MODEL), D_MODEL ** -0.5),
        "b_ada": nrm(ks[3], (L, N_MOD * D_MODEL), 0.02),
        "norm1_g": 1.0 + nrm(ks[4], (L, D_MODEL), 0.02),
        "w_in": nrm(ks[5], (L, D_MODEL, IN_TOTAL), D_MODEL ** -0.5),
        "w_gk2": nrm(ks[6], (L, GLA_GATE_RANK, GLA_KEY_W), GLA_GATE_RANK ** -0.5),
        "b_gk": nrm(ks[7], (L, GLA_KEY_W), 0.1),
        "gla_norm_g": 1.0 + nrm(ks[8], (L, GLA_DV), 0.02),
        "sink": nrm(ks[9], (L, SWA_HEADS), 1.0),
        "w_o": nrm(ks[10], (L, D_MODEL, D_MODEL), D_MODEL ** -0.5),
        "norm2_g": 1.0 + nrm(ks[11], (L, D_MODEL), 0.02),
        "w_group": nrm(ks[12], (L, D_MODEL, N_GROUPS), D_MODEL ** -0.5),
        "b_group": nrm(ks[13], (L, N_GROUPS), 0.01),
        "w_router": nrm(ks[14], (L, D_MODEL, N_EXPERTS), D_MODEL ** -0.5),
        "b_router": nrm(ks[15], (L, N_EXPERTS), 0.01),
        "w_gate": nrm(ks[16], (L, N_EXPERTS, D_MODEL, D_EXPERT), D_MODEL ** -0.5),
        "w_up": nrm(ks[17], (L, N_EXPERTS, D_MODEL, D_EXPERT), D_MODEL ** -0.5),
        "w_down": nrm(ks[18], (L, N_EXPERTS, D_EXPERT, D_MODEL), D_EXPERT ** -0.5),
        "norm_f_g": 1.0 + nrm(ks[19], (D_MODEL,), 0.02),
    }


def reference(x, c, w_ada, b_ada, norm1_g, w_in, w_gk2, b_gk, gla_norm_g, sink, w_o, norm2_g,
              w_group, b_group, w_router, b_router, w_gate, w_up, w_down, norm_f_g):
    for l in range(DEPTH):
        mod = jax.nn.silu(c) @ w_ada[l] + b_ada[l]
        sh1, sc1, gt1, sh2, sc2, gt2 = jnp.split(mod[:, None, :], N_MOD, axis=-1)
        h = rms_norm(x, norm1_g[l]) * (1.0 + sc1) + sh1
        x = x + gt1 * mixer(h, w_in[l], w_gk2[l], b_gk[l], gla_norm_g[l], sink[l], w_o[l])
        h = rms_norm(x, norm2_g[l]) * (1.0 + sc2) + sh2
        x = x + gt2 * hier_moe(h, w_group[l], b_group[l], w_router[l], b_router[l],
                               w_gate[l], w_up[l], w_down[l])
    return rms_norm(x, norm_f_g)
```

```python
import functools

import jax
import jax.numpy as jnp
from jax import lax
from jax.experimental import pallas as pl
from jax.experimental.pallas import tpu as pltpu

F32 = jnp.float32
BF16 = jnp.bfloat16

D_MODEL = 1024
GLA_HEADS = 4
GLA_DK = 128
GLA_DV = 256
GLA_KEY_W = GLA_HEADS * GLA_DK
GLA_VAL_W = GLA_HEADS * GLA_DV
GLA_GATE_RANK = 16
GLA_TAU = 16.0
GLA_CHUNK = 64
SWA_HEADS = 16
SWA_KV_HEADS = 2
SWA_GROUP = SWA_HEADS // SWA_KV_HEADS
SWA_DH = 64
SWA_W = SWA_HEADS * SWA_DH
SWA_KV_W = SWA_KV_HEADS * SWA_DH
WINDOW = 128
N_GROUPS = 4
EXPERTS_PER_GROUP = 4
N_EXPERTS = N_GROUPS * EXPERTS_PER_GROUP
D_EXPERT = 256
N_MOD = 6
EPS = 1e-6
IN_WIDTHS = (GLA_KEY_W, GLA_KEY_W, GLA_VAL_W, GLA_VAL_W, GLA_GATE_RANK,
             SWA_W, SWA_KV_W, SWA_KV_W, D_MODEL, D_MODEL)

LANES = 128
VMEM_LIMIT = 56 * 1024 * 1024

TM_PROJ = 256
TS_GLA = 256
TM_MERGE = 512
TM_MOE = 256
TN_MOD = 512


def _params(sem):
    return pltpu.CompilerParams(dimension_semantics=sem, vmem_limit_bytes=VMEM_LIMIT)


def _split(a):
    hi = a.astype(BF16)
    lo = (a - hi.astype(F32)).astype(BF16)
    return hi, lo


def _dot(a, b):
    return jnp.dot(a, b, preferred_element_type=F32)


def _dot_nt(a, b):
    return lax.dot_general(a, b, (((1,), (1,)), ((), ())), preferred_element_type=F32)


def _dot_tn(a, b):
    return lax.dot_general(a, b, (((0,), (0,)), ((), ())), preferred_element_type=F32)


def _dot3(a, b):
    a_hi, a_lo = _split(a)
    b_hi, b_lo = _split(b)
    return _dot(a_hi, b_hi) + _dot(a_hi, b_lo) + _dot(a_lo, b_hi)


def _sigmoid(x):
    return 1.0 / (1.0 + jnp.exp(-x))


def _silu(x):
    return x * _sigmoid(x)


def _rms(x, g):
    return x * lax.rsqrt(jnp.mean(x * x, axis=-1, keepdims=True) + EPS) * g


def _mod_kernel(c_ref, w_ref, b_ref, o_ref):
    o_ref[...] = _dot3(_silu(c_ref[...]), w_ref[...]) + b_ref[...]


def _mod_call(c, w_ada, b_ada):
    bsz = c.shape[0]
    n = w_ada.shape[1]
    return pl.pallas_call(
        _mod_kernel,
        grid=(n // TN_MOD,),
        in_specs=[pl.BlockSpec((bsz, D_MODEL), lambda j: (0, 0)),
                  pl.BlockSpec((D_MODEL, TN_MOD), lambda j: (0, j)),
                  pl.BlockSpec((1, TN_MOD), lambda j: (0, j))],
        out_specs=pl.BlockSpec((bsz, TN_MOD), lambda j: (0, j)),
        out_shape=jax.ShapeDtypeStruct((bsz, n), F32),
        compiler_params=_params(("parallel",)),
        name="mod",
    )(c, w_ada, b_ada.reshape(1, n))


def _inproj_kernel(x_ref, g_ref, sc_ref, sh_ref, *refs):
    n = len(refs) // 2
    w_refs, o_refs = refs[:n], refs[n:]
    h = _rms(x_ref[...], g_ref[...]) * (1.0 + sc_ref[...]) + sh_ref[...]
    hb = h.astype(BF16)
    for w_ref, o_ref in zip(w_refs, o_refs):
        o_ref[...] = _dot(hb, w_ref[...]).astype(o_ref.dtype)


def _inproj_call(x2, g1, sc1, sh1, weights, out_dtypes, seq):
    t = x2.shape[0]
    per_b = seq // TM_PROJ
    vec = pl.BlockSpec((None, 1, D_MODEL), lambda i: (i // per_b, 0, 0))
    in_specs = [pl.BlockSpec((TM_PROJ, D_MODEL), lambda i: (i, 0)),
                pl.BlockSpec((1, D_MODEL), lambda i: (0, 0)), vec, vec]
    in_specs += [pl.BlockSpec(w.shape, lambda i: (0, 0)) for w in weights]
    out_specs = [pl.BlockSpec((TM_PROJ, w.shape[1]), lambda i: (i, 0)) for w in weights]
    out_shape = [jax.ShapeDtypeStruct((t, w.shape[1]), dt) for w, dt in zip(weights, out_dtypes)]
    return pl.pallas_call(
        _inproj_kernel,
        grid=(t // TM_PROJ,),
        in_specs=in_specs,
        out_specs=out_specs,
        out_shape=out_shape,
        compiler_params=_params(("parallel",)),
        name="inproj",
    )(x2, g1, sc1, sh1, *weights)


def _gla_kernel(q_ref, k_ref, v_ref, r_ref, z_ref, w2_ref, bgk_ref, gn_ref, o_ref, st_ref):
    ts = q_ref.shape[0]

    @pl.when(pl.program_id(1) == 0)
    def _():
        st_ref[...] = jnp.zeros_like(st_ref)

    gz = _dot3(z_ref[...], w2_ref[...]) + bgk_ref[...]
    la = (jnp.minimum(gz, 0.0) - jnp.log(1.0 + jnp.exp(-jnp.abs(gz)))) * (1.0 / GLA_TAU)

    row = lax.broadcasted_iota(jnp.int32, (ts, ts), 0)
    col = lax.broadcasted_iota(jnp.int32, (ts, ts), 1)
    same = (row >> 6) == (col >> 6)
    causal = same & (col <= row)
    tri = causal.astype(BF16)
    ones = same.astype(BF16)
    la_hi, la_lo = _split(la)
    bc = _dot(tri, la_hi) + _dot(tri, la_lo)
    bt = _dot(ones, la_hi) + _dot(ones, la_lo)

    qf = q_ref[...].astype(F32) * (GLA_DK ** -0.5)
    kf = k_ref[...].astype(F32)
    q_dec = (qf * jnp.exp(bc)).astype(BF16)
    k_inv = (kf * jnp.exp(-bc)).astype(BF16)
    k_end = (kf * jnp.exp(bt - bc)).astype(BF16)
    dec = jnp.exp(bt)

    gn = gn_ref[...]
    for h in range(GLA_HEADS):
        ks = slice(h * GLA_DK, (h + 1) * GLA_DK)
        vs = slice(h * GLA_DV, (h + 1) * GLA_DV)
        qh = q_dec[:, ks]
        att = jnp.where(causal, _dot_nt(qh, k_inv[:, ks]), 0.0).astype(BF16)
        vh = v_ref[:, vs]
        o = _dot(att, vh)
        st = st_ref[h]
        inter = []
        for c in range(ts // GLA_CHUNK):
            rs = slice(c * GLA_CHUNK, (c + 1) * GLA_CHUNK)
            inter.append(_dot_nt(qh[rs], st.astype(BF16)))
            upd = _dot_tn(vh[rs], k_end[rs, ks])
            st = st * dec[c * GLA_CHUNK:c * GLA_CHUNK + 1, ks] + upd
        st_ref[h] = st
        o = o + jnp.concatenate(inter, axis=0)
        o = o * lax.rsqrt(jnp.mean(o * o, axis=-1, keepdims=True) + EPS) * gn
        o_ref[:, vs] = (o * _silu(r_ref[:, vs].astype(F32))).astype(o_ref.dtype)


def _gla_call(qa, ka, va, ra, z, w2p, bgk, gn, bsz, seq):
    ns = seq // TS_GLA
    tok = lambda w: pl.BlockSpec((TS_GLA, w), lambda b, s: (b * ns + s, 0))
    full = lambda a: pl.BlockSpec(a.shape, lambda b, s: (0, 0))
    return pl.pallas_call(
        _gla_kernel,
        grid=(bsz, ns),
        in_specs=[tok(GLA_KEY_W), tok(GLA_KEY_W), tok(GLA_VAL_W), tok(GLA_VAL_W), tok(LANES),
                  full(w2p), full(bgk), full(gn)],
        out_specs=tok(GLA_VAL_W),
        out_shape=jax.ShapeDtypeStruct((bsz * seq, GLA_VAL_W), BF16),
        scratch_shapes=[pltpu.VMEM((GLA_HEADS, GLA_DV, GLA_DK), F32)],
        compiler_params=_params(("parallel", "arbitrary")),
        name="gla",
    )(qa, ka, va, ra, z, w2p, bgk, gn)


def _swa_kernel(q_ref, kc_ref, kp_ref, vc_ref, vp_ref, sink_ref, o_ref):
    first = pl.program_id(1) == 0
    w = WINDOW
    lane = lax.broadcasted_iota(jnp.int32, (2 * w, LANES), 1)
    lo_half = lane < SWA_DH

    kk = jnp.concatenate([kp_ref[...], kc_ref[...]], axis=0)
    vv = jnp.concatenate([vp_ref[...], vc_ref[...]], axis=0)
    kk_sw = pltpu.roll(kk, SWA_DH, 1)
    vv_sw = pltpu.roll(vv, SWA_DH, 1)
    k2 = (jnp.where(lo_half, kk, kk_sw), jnp.where(lo_half, kk_sw, kk))
    v2 = (jnp.where(lo_half, vv, vv_sw), jnp.where(lo_half, vv_sw, vv))

    qi = lax.broadcasted_iota(jnp.int32, (w, 2 * w), 0) + w
    kj = lax.broadcasted_iota(jnp.int32, (w, 2 * w), 1)
    rel = qi - kj
    valid = (rel >= 0) & (rel < w) & (jnp.logical_not(first) | (kj >= w))

    qlane = lax.broadcasted_iota(jnp.int32, (w, LANES), 1)
    q_lo = qlane < SWA_DH
    sink = sink_ref[...]
    for p in range(SWA_HEADS // 2):
        hk = (2 * p) // SWA_GROUP
        qp = q_ref[:, p * LANES:(p + 1) * LANES]
        halves = []
        for j in range(2):
            head = 2 * p + j
            qm = jnp.where(q_lo if j == 0 else jnp.logical_not(q_lo), qp, jnp.zeros_like(qp))
            s = _dot_nt(qm, k2[hk]) * (SWA_DH ** -0.5)
            s = jnp.where(valid, s, -jnp.inf)
            sk = sink[:, head:head + 1]
            m = jnp.maximum(jnp.max(s, axis=-1, keepdims=True), sk)
            e = jnp.exp(s - m)
            den = jnp.sum(e, axis=-1, keepdims=True) + jnp.exp(sk - m)
            pr = (e / den).astype(BF16)
            halves.append(_dot(pr, v2[hk]))
        o_ref[:, p * LANES:(p + 1) * LANES] = jnp.where(q_lo, halves[0], halves[1]).astype(o_ref.dtype)


def _swa_call(qb, kb, vb, sink, bsz, seq):
    nb = seq // WINDOW
    cur = lambda w: pl.BlockSpec((WINDOW, w), lambda b, n: (b * nb + n, 0))
    prev = lambda w: pl.BlockSpec((WINDOW, w), lambda b, n: (b * nb + jnp.maximum(n - 1, 0), 0))
    return pl.pallas_call(
        _swa_kernel,
        grid=(bsz, nb),
        in_specs=[cur(SWA_W), cur(SWA_KV_W), prev(SWA_KV_W), cur(SWA_KV_W), prev(SWA_KV_W),
                  pl.BlockSpec((1, SWA_HEADS), lambda b, n: (0, 0))],
        out_specs=cur(SWA_W),
        out_shape=jax.ShapeDtypeStruct((bsz * seq, SWA_W), BF16),
        compiler_params=_params(("parallel", "arbitrary")),
        name="swa",
    )(qb, kb, kb, vb, vb, sink)


def _merge_kernel(x_ref, oa_ref, ob_ref, ga_ref, gb_ref, gt_ref, wo_ref, o_ref):
    merged = (_sigmoid(ga_ref[...].astype(F32)) * oa_ref[...].astype(F32)
              + _sigmoid(gb_ref[...].astype(F32)) * ob_ref[...].astype(F32))
    o_ref[...] = x_ref[...] + gt_ref[...] * _dot(merged.astype(BF16), wo_ref[...])


def _merge_call(x2, oa, ob, ga, gb, gt1, wo, seq):
    t = x2.shape[0]
    per_b = seq // TM_MERGE
    tok = pl.BlockSpec((TM_MERGE, D_MODEL), lambda i: (i, 0))
    return pl.pallas_call(
        _merge_kernel,
        grid=(t // TM_MERGE,),
        in_specs=[tok, tok, tok, tok, tok,
                  pl.BlockSpec((None, 1, D_MODEL), lambda i: (i // per_b, 0, 0)),
                  pl.BlockSpec((D_MODEL, D_MODEL), lambda i: (0, 0))],
        out_specs=tok,
        out_shape=jax.ShapeDtypeStruct((t, D_MODEL), F32),
        compiler_params=_params(("parallel",)),
        name="merge",
    )(x2, oa, ob, ga, gb, gt1, wo)


def _first_argmax(v, lane, vmax):
    return jnp.min(jnp.where(v == vmax, lane, LANES), axis=-1, keepdims=True)


def _moe_kernel(x_ref, g2_ref, sc_ref, sh_ref, gt_ref, wr_ref, br_ref, wg_ref, wu_ref, wd_ref, gf_ref, o_ref):
    x1 = x_ref[...]
    h = _rms(x1, g2_ref[...]) * (1.0 + sc_ref[...]) + sh_ref[...]
    hb = h.astype(BF16)
    tm = x1.shape[0]

    logits = _dot(hb, wr_ref[...]) + br_ref[...]
    lane = lax.broadcasted_iota(jnp.int32, (tm, LANES), 1)
    neg = -jnp.inf
    lg = jnp.where(lane < N_GROUPS, logits, neg)
    gmax = jnp.max(lg, axis=-1, keepdims=True)
    g_w = 1.0 / jnp.sum(jnp.exp(lg - gmax), axis=-1, keepdims=True)
    g_idx = _first_argmax(lg, lane, gmax)
    e_lane = lane - N_GROUPS
    in_grp = (e_lane >= 0) & (e_lane < N_EXPERTS) & ((e_lane >> 2) == g_idx)
    le = jnp.where(in_grp, logits, neg)
    v1 = jnp.max(le, axis=-1, keepdims=True)
    i1 = _first_argmax(le, lane, v1)
    le2 = jnp.where(lane == i1, neg, le)
    v2 = jnp.max(le2, axis=-1, keepdims=True)
    i2 = _first_argmax(le2, lane, v2)
    e2 = jnp.exp(v2 - v1)
    w1 = g_w / (1.0 + e2)
    w2 = g_w * e2 / (1.0 + e2)
    combine = jnp.where(lane == i1, w1, 0.0) + jnp.where(lane == i2, w2, 0.0)

    y = jnp.zeros((tm, D_MODEL), F32)
    for e in range(N_EXPERTS):
        hid = _silu(_dot(hb, wg_ref[e])) * _dot(hb, wu_ref[e])
        hid = hid * combine[:, N_GROUPS + e:N_GROUPS + e + 1]
        y = y + _dot(hid.astype(BF16), wd_ref[e])
    o_ref[...] = _rms(x1 + gt_ref[...] * y, gf_ref[...])


def _moe_call(x1, g2, sc2, sh2, gt2, wr, br, wg, wu, wd, gf, seq):
    t = x1.shape[0]
    per_b = seq // TM_MOE
    tok = pl.BlockSpec((TM_MOE, D_MODEL), lambda i: (i, 0))
    vec = pl.BlockSpec((None, 1, D_MODEL), lambda i: (i // per_b, 0, 0))
    row = pl.BlockSpec((1, D_MODEL), lambda i: (0, 0))
    full = lambda a: pl.BlockSpec(a.shape, lambda i: (0,) * a.ndim)
    return pl.pallas_call(
        _moe_kernel,
        grid=(t // TM_MOE,),
        in_specs=[tok, row, vec, vec, vec, full(wr), full(br), full(wg), full(wu), full(wd), row],
        out_specs=tok,
        out_shape=jax.ShapeDtypeStruct((t, D_MODEL), F32),
        compiler_params=_params(("parallel",)),
        name="moe",
    )(x1, g2, sc2, sh2, gt2, wr, br, wg, wu, wd, gf)


def kernel(x, c, w_ada, b_ada, norm1_g, w_in, w_gk2, b_gk, gla_norm_g, sink, w_o, norm2_g,
           w_group, b_group, w_router, b_router, w_gate, w_up, w_down, norm_f_g):
    bsz, seq, d = x.shape
    depth = w_ada.shape[0]
    assert depth == 1, "the final norm is fused into the (single) layer's MoE kernel"
    x2 = x.reshape(bsz * seq, d)
    for l in range(depth):
        mod = _mod_call(c, w_ada[l], b_ada[l])
        sh1, sc1, gt1, sh2, sc2, gt2 = [m.reshape(bsz, 1, d) for m in jnp.split(mod, N_MOD, axis=-1)]

        w_parts = _split_cols(w_in[l])
        wz = jnp.pad(w_parts[4], ((0, 0), (0, LANES - GLA_GATE_RANK)))
        weights = [w.astype(BF16) for w in w_parts[:4]] + [wz.astype(BF16)] + [w.astype(BF16) for w in w_parts[5:]]
        out_dtypes = [BF16] * 4 + [F32] + [BF16] * 5
        qa, ka, va, ra, z, qb, kb, vb, ga, gb = _inproj_call(
            x2, norm1_g[l].reshape(1, d), sc1, sh1, weights, out_dtypes, seq)

        w2p = jnp.pad(w_gk2[l], ((0, LANES - GLA_GATE_RANK), (0, 0)))
        oa = _gla_call(qa, ka, va, ra, z, w2p, b_gk[l].reshape(1, GLA_KEY_W),
                       gla_norm_g[l].reshape(1, GLA_DV), bsz, seq)
        ob = _swa_call(qb, kb, vb, sink[l].reshape(1, SWA_HEADS), bsz, seq)
        x2 = _merge_call(x2, oa, ob, ga, gb, gt1, w_o[l].astype(BF16), seq)

        wr = jnp.pad(jnp.concatenate([w_group[l], w_router[l]], axis=1),
                     ((0, 0), (0, LANES - N_GROUPS - N_EXPERTS))).astype(BF16)
        br = jnp.pad(jnp.concatenate([b_group[l], b_router[l]]), (0, LANES - N_GROUPS - N_EXPERTS)).reshape(1, LANES)
        gf = norm_f_g.reshape(1, d)
        x2 = _moe_call(x2, norm2_g[l].reshape(1, d), sc2, sh2, gt2, wr, br,
                       w_gate[l].astype(BF16), w_up[l].astype(BF16), w_down[l].astype(BF16), gf, seq)
    return x2.reshape(bsz, seq, d)


def _split_cols(w):
    parts, start = [], 0
    for width in IN_WIDTHS:
        parts.append(w[:, start:start + width])
        start += width
    return parts
```

```python
import functools

import jax
import jax.numpy as jnp
from jax import lax
from jax.experimental import pallas as pl
from jax.experimental.pallas import tpu as pltpu

F32 = jnp.float32
BF16 = jnp.bfloat16

D_MODEL = 1024
GLA_HEADS = 4
GLA_DK = 128
GLA_DV = 256
GLA_KEY_W = GLA_HEADS * GLA_DK
GLA_VAL_W = GLA_HEADS * GLA_DV
GLA_GATE_RANK = 16
GLA_TAU = 16.0
GLA_CHUNK = 64
SWA_HEADS = 16
SWA_KV_HEADS = 2
SWA_GROUP = SWA_HEADS // SWA_KV_HEADS
SWA_DH = 64
SWA_W = SWA_HEADS * SWA_DH
SWA_KV_W = SWA_KV_HEADS * SWA_DH
WINDOW = 128
N_GROUPS = 4
EXPERTS_PER_GROUP = 4
N_EXPERTS = N_GROUPS * EXPERTS_PER_GROUP
D_EXPERT = 256
N_MOD = 6
EPS = 1e-6
IN_WIDTHS = (GLA_KEY_W, GLA_KEY_W, GLA_VAL_W, GLA_VAL_W, GLA_GATE_RANK,
             SWA_W, SWA_KV_W, SWA_KV_W, D_MODEL, D_MODEL)

LANES = 128
VMEM_LIMIT = 56 * 1024 * 1024

TM_PROJ = 256
TS_GLA = 256
TQ_SWA = 256
TM_MERGE = 512
TM_MOE = 512
ROW_PACK = 16
MOE_PAD = 128
MOE_BM = 144
TN_MOD = 512


def _params(sem):
    return pltpu.CompilerParams(dimension_semantics=sem, vmem_limit_bytes=VMEM_LIMIT)


def _split(a):
    hi = a.astype(BF16)
    lo = (a - hi.astype(F32)).astype(BF16)
    return hi, lo


def _dot(a, b):
    return jnp.dot(a, b, preferred_element_type=F32)


def _dot_nt(a, b):
    return lax.dot_general(a, b, (((1,), (1,)), ((), ())), preferred_element_type=F32)


def _dot_tn(a, b):
    return lax.dot_general(a, b, (((0,), (0,)), ((), ())), preferred_element_type=F32)


def _dot3(a, b):
    a_hi, a_lo = _split(a)
    b_hi, b_lo = _split(b)
    return _dot(a_hi, b_hi) + _dot(a_hi, b_lo) + _dot(a_lo, b_hi)


def _sigmoid(x):
    return 1.0 / (1.0 + jnp.exp(-x))


def _silu(x):
    return x * _sigmoid(x)


def _rms(x, g):
    return x * lax.rsqrt(jnp.mean(x * x, axis=-1, keepdims=True) + EPS) * g


def _mod_kernel(c_ref, w_ref, b_ref, o_ref):
    o_ref[...] = _dot3(_silu(c_ref[...]), w_ref[...]) + b_ref[...]


def _mod_call(c, w_ada, b_ada):
    bsz = c.shape[0]
    n = w_ada.shape[1]
    return pl.pallas_call(
        _mod_kernel,
        grid=(n // TN_MOD,),
        in_specs=[pl.BlockSpec((bsz, D_MODEL), lambda j: (0, 0)),
                  pl.BlockSpec((D_MODEL, TN_MOD), lambda j: (0, j)),
                  pl.BlockSpec((1, TN_MOD), lambda j: (0, j))],
        out_specs=pl.BlockSpec((bsz, TN_MOD), lambda j: (0, j)),
        out_shape=jax.ShapeDtypeStruct((bsz, n), F32),
        compiler_params=_params(("parallel",)),
        name="mod",
    )(c, w_ada, b_ada.reshape(1, n))


def _inproj_kernel(x_ref, g_ref, sc_ref, sh_ref, *refs):
    n = len(refs) // 2
    w_refs, o_refs = refs[:n], refs[n:]
    h = _rms(x_ref[...], g_ref[...]) * (1.0 + sc_ref[...]) + sh_ref[...]
    hb = h.astype(BF16)
    for w_ref, o_ref in zip(w_refs, o_refs):
        o_ref[...] = _dot(hb, w_ref[...]).astype(o_ref.dtype)


def _inproj_call(x2, g1, sc1, sh1, weights, out_dtypes, seq):
    t = x2.shape[0]
    per_b = seq // TM_PROJ
    vec = pl.BlockSpec((None, 1, D_MODEL), lambda i: (i // per_b, 0, 0))
    in_specs = [pl.BlockSpec((TM_PROJ, D_MODEL), lambda i: (i, 0)),
                pl.BlockSpec((1, D_MODEL), lambda i: (0, 0)), vec, vec]
    in_specs += [pl.BlockSpec(w.shape, lambda i: (0, 0)) for w in weights]
    out_specs = [pl.BlockSpec((TM_PROJ, w.shape[1]), lambda i: (i, 0)) for w in weights]
    out_shape = [jax.ShapeDtypeStruct((t, w.shape[1]), dt) for w, dt in zip(weights, out_dtypes)]
    return pl.pallas_call(
        _inproj_kernel,
        grid=(t // TM_PROJ,),
        in_specs=in_specs,
        out_specs=out_specs,
        out_shape=out_shape,
        compiler_params=_params(("parallel",)),
        name="inproj",
    )(x2, g1, sc1, sh1, *weights)


def _gla_kernel(q_ref, k_ref, v_ref, r_ref, z_ref, w2_ref, bgk_ref, gn_ref, o_ref, st_ref):
    ts = q_ref.shape[0]

    @pl.when(pl.program_id(1) == 0)
    def _():
        st_ref[...] = jnp.zeros_like(st_ref)

    gz = _dot3(z_ref[...], w2_ref[...]) + bgk_ref[...]
    la = (jnp.minimum(gz, 0.0) - jnp.log(1.0 + jnp.exp(-jnp.abs(gz)))) * (1.0 / GLA_TAU)

    row = lax.broadcasted_iota(jnp.int32, (ts, ts), 0)
    col = lax.broadcasted_iota(jnp.int32, (ts, ts), 1)
    same = (row >> 6) == (col >> 6)
    causal = same & (col <= row)
    tri = causal.astype(BF16)
    ones = same.astype(BF16)
    la_hi, la_lo = _split(la)
    bc = _dot(tri, la_hi) + _dot(tri, la_lo)
    bt = _dot(ones, la_hi) + _dot(ones, la_lo)

    qf = q_ref[...].astype(F32) * (GLA_DK ** -0.5)
    kf = k_ref[...].astype(F32)
    q_dec = (qf * jnp.exp(bc)).astype(BF16)
    k_inv = (kf * jnp.exp(-bc)).astype(BF16)
    k_end = (kf * jnp.exp(bt - bc)).astype(BF16)
    dec = jnp.exp(bt)

    gn = gn_ref[...]
    for h in range(GLA_HEADS):
        ks = slice(h * GLA_DK, (h + 1) * GLA_DK)
        vs = slice(h * GLA_DV, (h + 1) * GLA_DV)
        qh = q_dec[:, ks]
        att = jnp.where(causal, _dot_nt(qh, k_inv[:, ks]), 0.0).astype(BF16)
        vh = v_ref[:, vs]
        o = _dot(att, vh)
        st = st_ref[h]
        inter = []
        for c in range(ts // GLA_CHUNK):
            rs = slice(c * GLA_CHUNK, (c + 1) * GLA_CHUNK)
            inter.append(_dot_nt(qh[rs], st.astype(BF16)))
            upd = _dot_tn(vh[rs], k_end[rs, ks])
            st = st * dec[c * GLA_CHUNK:c * GLA_CHUNK + 1, ks] + upd
        st_ref[h] = st
        o = o + jnp.concatenate(inter, axis=0)
        o = o * lax.rsqrt(jnp.mean(o * o, axis=-1, keepdims=True) + EPS) * gn
        o_ref[:, vs] = (o * _silu(r_ref[:, vs].astype(F32))).astype(o_ref.dtype)


def _gla_call(qa, ka, va, ra, z, w2p, bgk, gn, bsz, seq):
    ns = seq // TS_GLA
    tok = lambda w: pl.BlockSpec((TS_GLA, w), lambda b, s: (b * ns + s, 0))
    full = lambda a: pl.BlockSpec(a.shape, lambda b, s: (0, 0))
    return pl.pallas_call(
        _gla_kernel,
        grid=(bsz, ns),
        in_specs=[tok(GLA_KEY_W), tok(GLA_KEY_W), tok(GLA_VAL_W), tok(GLA_VAL_W), tok(LANES),
                  full(w2p), full(bgk), full(gn)],
        out_specs=tok(GLA_VAL_W),
        out_shape=jax.ShapeDtypeStruct((bsz * seq, GLA_VAL_W), BF16),
        scratch_shapes=[pltpu.VMEM((GLA_HEADS, GLA_DV, GLA_DK), F32)],
        compiler_params=_params(("parallel", "arbitrary")),
        name="gla",
    )(qa, ka, va, ra, z, w2p, bgk, gn)


def _swa_kernel(q_ref, kc_ref, kp_ref, vc_ref, vp_ref, fill_ref, o_ref):
    w = WINDOW
    tq = q_ref.shape[0]
    first = pl.program_id(1) == 0
    zero_prev = lambda a: jnp.where(first, jnp.zeros_like(a), a)
    k_all = jnp.concatenate([zero_prev(kp_ref[...]), kc_ref[...]], axis=0)
    v_all = jnp.concatenate([zero_prev(vp_ref[...]), vc_ref[...]], axis=0)
    lane = lax.broadcasted_iota(jnp.int32, (w + tq, LANES), 1)
    lo_half = lane < SWA_DH
    k_sw = pltpu.roll(k_all, SWA_DH, 1)
    v_sw = pltpu.roll(v_all, SWA_DH, 1)
    k2 = (jnp.where(lo_half, k_all, k_sw), jnp.where(lo_half, k_sw, k_all))
    v2 = (jnp.where(lo_half, v_all, v_sw), jnp.where(lo_half, v_sw, v_all))

    qi = lax.broadcasted_iota(jnp.int32, (w, 2 * w), 0) + w
    kj = lax.broadcasted_iota(jnp.int32, (w, 2 * w), 1)
    rel = qi - kj
    band = (rel >= 0) & (rel < w)
    band_first = band & (jnp.logical_not(first) | (kj >= w))
    q_lo = lax.broadcasted_iota(jnp.int32, (w, LANES), 1) < SWA_DH
    row0 = lax.broadcasted_iota(jnp.int32, (2 * w, LANES), 0) == 0
    ones = jnp.ones((2 * w, LANES), BF16)
    scale = jnp.asarray(SWA_DH ** -0.5, BF16)

    for blk in range(tq // w):
        valid = band_first if blk == 0 else band
        qs = slice(blk * w, (blk + 1) * w)
        ws = slice(blk * w, blk * w + 2 * w)
        for hk in range(SWA_KV_HEADS):
            parts = []
            for p in range(SWA_GROUP // 2):
                cs = slice((hk * (SWA_GROUP // 2) + p) * LANES, (hk * (SWA_GROUP // 2) + p + 1) * LANES)
                qp = q_ref[qs, cs] * scale
                parts += [jnp.where(q_lo, qp, jnp.zeros_like(qp)), jnp.where(q_lo, jnp.zeros_like(qp), qp)]
            s = _dot_nt(jnp.concatenate(parts, axis=0), k2[hk][ws])
            es = []
            for r in range(SWA_GROUP):
                head = hk * SWA_GROUP + r
                sh = jnp.where(valid, s[r * w:(r + 1) * w], fill_ref[head:head + 1, :])
                es.append(jnp.exp(sh - jnp.max(sh, axis=-1, keepdims=True)).astype(BF16))
            vv = jnp.where(row0, jnp.zeros_like(ones), v2[hk][ws])
            o = _dot(jnp.concatenate(es, axis=0), jnp.concatenate([vv, ones], axis=1))
            for p in range(SWA_GROUP // 2):
                cs = slice((hk * (SWA_GROUP // 2) + p) * LANES, (hk * (SWA_GROUP // 2) + p + 1) * LANES)
                oe = o[(2 * p) * w:(2 * p + 1) * w]
                oo = o[(2 * p + 1) * w:(2 * p + 2) * w]
                num = jnp.where(q_lo, oe[:, :LANES], oo[:, :LANES])
                den = jnp.where(q_lo, oe[:, LANES:], oo[:, LANES:])
                o_ref[qs, cs] = (num / den).astype(o_ref.dtype)


def _swa_call(qb, kb, vb, fill, bsz, seq):
    nt = seq // TQ_SWA
    per = TQ_SWA // WINDOW
    cur = lambda w: pl.BlockSpec((TQ_SWA, w), lambda b, n: (b * nt + n, 0))
    prev = lambda w: pl.BlockSpec((WINDOW, w), lambda b, n: (jnp.maximum((b * nt + n) * per - 1, 0), 0))
    return pl.pallas_call(
        _swa_kernel,
        grid=(bsz, nt),
        in_specs=[cur(SWA_W), cur(SWA_KV_W), prev(SWA_KV_W), cur(SWA_KV_W), prev(SWA_KV_W),
                  pl.BlockSpec(fill.shape, lambda b, n: (0, 0))],
        out_specs=cur(SWA_W),
        out_shape=jax.ShapeDtypeStruct((bsz * seq, SWA_W), BF16),
        compiler_params=_params(("parallel", "arbitrary")),
        name="swa",
    )(qb, kb, kb, vb, vb, fill)


def _merge_kernel(x_ref, oa_ref, ob_ref, ga_ref, gb_ref, gt_ref, wo_ref, o_ref):
    merged = (_sigmoid(ga_ref[...].astype(F32)) * oa_ref[...].astype(F32)
              + _sigmoid(gb_ref[...].astype(F32)) * ob_ref[...].astype(F32))
    o_ref[...] = x_ref[...] + gt_ref[...] * _dot(merged.astype(BF16), wo_ref[...])


def _merge_call(x2, oa, ob, ga, gb, gt1, wo, seq):
    t = x2.shape[0]
    per_b = seq // TM_MERGE
    tok = pl.BlockSpec((TM_MERGE, D_MODEL), lambda i: (i, 0))
    return pl.pallas_call(
        _merge_kernel,
        grid=(t // TM_MERGE,),
        in_specs=[tok, tok, tok, tok, tok,
                  pl.BlockSpec((None, 1, D_MODEL), lambda i: (i // per_b, 0, 0)),
                  pl.BlockSpec((D_MODEL, D_MODEL), lambda i: (0, 0))],
        out_specs=tok,
        out_shape=jax.ShapeDtypeStruct((t, D_MODEL), F32),
        compiler_params=_params(("parallel",)),
        name="merge",
    )(x2, oa, ob, ga, gb, gt1, wo)


def _first_argmax(v, lane, vmax):
    return jnp.min(jnp.where(v == vmax, lane, LANES), axis=-1, keepdims=True)


def _moe_kernel(x_ref, g2_ref, sc_ref, sh_ref, gt_ref, wr_ref, br_ref, wg_ref, wu_ref, wd_ref, gf_ref, o_ref,
                hs_ref, cws_ref, ys_ref):
    x1 = x_ref[...]
    h = _rms(x1, g2_ref[...]) * (1.0 + sc_ref[...]) + sh_ref[...]
    hb = h.astype(BF16)
    tm = x1.shape[0]

    logits = _dot(hb, wr_ref[...]) + br_ref[...]
    lane = lax.broadcasted_iota(jnp.int32, (tm, LANES), 1)
    neg = -jnp.inf
    lg = jnp.where(lane < N_GROUPS, logits, neg)
    gmax = jnp.max(lg, axis=-1, keepdims=True)
    g_w = 1.0 / jnp.sum(jnp.exp(lg - gmax), axis=-1, keepdims=True)
    g_idx = _first_argmax(lg, lane, gmax)
    e_lane = lane - N_GROUPS
    in_grp = (e_lane >= 0) & (e_lane < N_EXPERTS) & ((e_lane >> 2) == g_idx)
    le = jnp.where(in_grp, logits, neg)
    v1 = jnp.max(le, axis=-1, keepdims=True)
    i1 = _first_argmax(le, lane, v1)
    le2 = jnp.where(lane == i1, neg, le)
    v2 = jnp.max(le2, axis=-1, keepdims=True)
    i2 = _first_argmax(le2, lane, v2)
    e2 = jnp.exp(v2 - v1)
    w1 = g_w / (1.0 + e2)
    w2 = g_w * e2 / (1.0 + e2)
    combine = jnp.where(lane == i1, w1, 0.0) + jnp.where(lane == i2, w2, 0.0)

    onehot = (lane == g_idx).astype(F32)
    r_i = lax.broadcasted_iota(jnp.int32, (tm, tm), 0)
    c_i = lax.broadcasted_iota(jnp.int32, (tm, tm), 1)
    rank = _dot((c_i < r_i).astype(BF16), onehot.astype(BF16))
    cnt = jnp.sum(onehot, axis=0, keepdims=True)
    pcs, offs, off_row, run = [], [], jnp.zeros((1, LANES), F32), 0
    lane_row = lax.broadcasted_iota(jnp.int32, (1, LANES), 1)
    for g in range(N_GROUPS):
        pc = ((cnt[0, g].astype(jnp.int32) + (ROW_PACK - 1)) // ROW_PACK) * ROW_PACK
        pcs.append(pc)
        offs.append(run)
        off_row = off_row + jnp.where(lane_row == g, jnp.asarray(run, F32), 0.0)
        run = run + pc
    pos = jnp.sum(onehot * (off_row + rank), axis=-1, keepdims=True).astype(jnp.int32)
    rows = tm + MOE_PAD
    sel_t = (lax.broadcasted_iota(jnp.int32, (tm, rows), 1) == pos).astype(BF16)
    hs_ref[0:rows, :] = _dot_tn(sel_t, hb).astype(BF16)
    c_hi, c_lo = _split(combine)
    cws_ref[0:rows, :] = _dot_tn(sel_t, c_hi) + _dot_tn(sel_t, c_lo)
    hs_ref[rows:, :] = jnp.zeros((MOE_BM, D_MODEL), BF16)
    cws_ref[rows:, :] = jnp.zeros((MOE_BM, LANES), F32)
    ys_ref[...] = jnp.zeros_like(ys_ref)

    for g in range(N_GROUPS):
        def block(k, carry, g=g):
            start = pl.multiple_of(offs[g] + k * MOE_BM, ROW_PACK)
            hblk = hs_ref[pl.ds(start, MOE_BM), :]
            cw = cws_ref[pl.ds(start, MOE_BM), :]
            hid = _silu(_dot(hblk, wg_ref[g])) * _dot(hblk, wu_ref[g])
            parts = []
            for e in range(EXPERTS_PER_GROUP):
                col = N_GROUPS + g * EXPERTS_PER_GROUP + e
                parts.append(hid[:, e * D_EXPERT:(e + 1) * D_EXPERT] * cw[:, col:col + 1])
            hid = jnp.concatenate(parts, axis=1).astype(BF16)
            ys_ref[pl.ds(start, MOE_BM), :] += _dot(hid, wd_ref[g])
            return carry
        lax.fori_loop(0, (pcs[g] + (MOE_BM - 1)) // MOE_BM, block, 0)

    y = _dot(sel_t, ys_ref[0:rows, :].astype(BF16))
    o_ref[...] = _rms(x1 + gt_ref[...] * y, gf_ref[...])


def _moe_body(x_ref, g2_ref, sc_ref, sh_ref, gt_ref, wr_ref, br_ref, wg_hbm, wu_hbm, wd_hbm, gf_ref, o_ref,
              wg_ref, wu_ref, wd_ref, hs_ref, cws_ref, ys_ref):
    @pl.when(pl.program_id(0) == 0)
    def _():
        pltpu.sync_copy(wg_hbm, wg_ref)
        pltpu.sync_copy(wu_hbm, wu_ref)
        pltpu.sync_copy(wd_hbm, wd_ref)

    _moe_kernel(x_ref, g2_ref, sc_ref, sh_ref, gt_ref, wr_ref, br_ref, wg_ref, wu_ref, wd_ref, gf_ref, o_ref,
                hs_ref, cws_ref, ys_ref)


def _moe_call(x1, g2, sc2, sh2, gt2, wr, br, wg, wu, wd, gf, seq):
    t = x1.shape[0]
    per_b = seq // TM_MOE
    tok = pl.BlockSpec((TM_MOE, D_MODEL), lambda i: (i, 0))
    vec = pl.BlockSpec((None, 1, D_MODEL), lambda i: (i // per_b, 0, 0))
    row = pl.BlockSpec((1, D_MODEL), lambda i: (0, 0))
    full = lambda a: pl.BlockSpec(a.shape, lambda i: (0,) * a.ndim)
    hbm = pl.BlockSpec(memory_space=pl.ANY)
    srows = TM_MOE + MOE_PAD + MOE_BM
    return pl.pallas_call(
        _moe_body,
        grid=(t // TM_MOE,),
        in_specs=[tok, row, vec, vec, vec, full(wr), full(br), hbm, hbm, hbm, row],
        out_specs=tok,
        out_shape=jax.ShapeDtypeStruct((t, D_MODEL), F32),
        scratch_shapes=[pltpu.VMEM(wg.shape, BF16), pltpu.VMEM(wu.shape, BF16), pltpu.VMEM(wd.shape, BF16),
                        pltpu.VMEM((srows, D_MODEL), BF16), pltpu.VMEM((srows, LANES), F32),
                        pltpu.VMEM((srows, D_MODEL), F32)],
        compiler_params=_params(("arbitrary",)),
        name="moe",
    )(x1, g2, sc2, sh2, gt2, wr, br, wg, wu, wd, gf)


def kernel(x, c, w_ada, b_ada, norm1_g, w_in, w_gk2, b_gk, gla_norm_g, sink, w_o, norm2_g,
           w_group, b_group, w_router, b_router, w_gate, w_up, w_down, norm_f_g):
    bsz, seq, d = x.shape
    depth = w_ada.shape[0]
    assert depth == 1, "the final norm is fused into the (single) layer's MoE kernel"
    x2 = x.reshape(bsz * seq, d)
    for l in range(depth):
        mod = _mod_call(c, w_ada[l], b_ada[l])
        sh1, sc1, gt1, sh2, sc2, gt2 = [m.reshape(bsz, 1, d) for m in jnp.split(mod, N_MOD, axis=-1)]

        w_parts = _split_cols(w_in[l])
        wz = jnp.pad(w_parts[4], ((0, 0), (0, LANES - GLA_GATE_RANK)))
        weights = [w.astype(BF16) for w in w_parts[:4]] + [wz.astype(BF16)] + [w.astype(BF16) for w in w_parts[5:]]
        out_dtypes = [BF16] * 4 + [F32] + [BF16] * 5
        qa, ka, va, ra, z, qb, kb, vb, ga, gb = _inproj_call(
            x2, norm1_g[l].reshape(1, d), sc1, sh1, weights, out_dtypes, seq)

        w2p = jnp.pad(w_gk2[l], ((0, LANES - GLA_GATE_RANK), (0, 0)))
        oa = _gla_call(qa, ka, va, ra, z, w2p, b_gk[l].reshape(1, GLA_KEY_W),
                       gla_norm_g[l].reshape(1, GLA_DV), bsz, seq)
        fill = jnp.full((SWA_HEADS, 2 * WINDOW), -jnp.inf, F32).at[:, 0].set(sink[l])
        ob = _swa_call(qb, kb, vb, fill, bsz, seq)
        x2 = _merge_call(x2, oa, ob, ga, gb, gt1, w_o[l].astype(BF16), seq)

        wr = jnp.pad(jnp.concatenate([w_group[l], w_router[l]], axis=1),
                     ((0, 0), (0, LANES - N_GROUPS - N_EXPERTS))).astype(BF16)
        br = jnp.pad(jnp.concatenate([b_group[l], b_router[l]]), (0, LANES - N_GROUPS - N_EXPERTS)).reshape(1, LANES)
        gf = norm_f_g.reshape(1, d)
        by_group = lambda w: w.astype(BF16).reshape(N_GROUPS, EXPERTS_PER_GROUP, d, D_EXPERT) \
            .transpose(0, 2, 1, 3).reshape(N_GROUPS, d, EXPERTS_PER_GROUP * D_EXPERT)
        wd = w_down[l].astype(BF16).reshape(N_GROUPS, EXPERTS_PER_GROUP * D_EXPERT, d)
        x2 = _moe_call(x2, norm2_g[l].reshape(1, d), sc2, sh2, gt2, wr, br,
                       by_group(w_gate[l]), by_group(w_up[l]), wd, gf, seq)
    return x2.reshape(bsz, seq, d)


def _split_cols(w):
    parts, start = [], 0
    for width in IN_WIDTHS:
        parts.append(w[:, start:start + width])
        start += width
    return parts
```

```python
import functools

import jax
import jax.numpy as jnp
from jax import lax
from jax.experimental import pallas as pl
from jax.experimental.pallas import tpu as pltpu

F32 = jnp.float32
BF16 = jnp.bfloat16

D_MODEL = 1024
GLA_HEADS = 4
GLA_DK = 128
GLA_DV = 256
GLA_KEY_W = GLA_HEADS * GLA_DK
GLA_VAL_W = GLA_HEADS * GLA_DV
GLA_GATE_RANK = 16
GLA_TAU = 16.0
GLA_CHUNK = 64
SWA_HEADS = 16
SWA_KV_HEADS = 2
SWA_GROUP = SWA_HEADS // SWA_KV_HEADS
SWA_DH = 64
SWA_W = SWA_HEADS * SWA_DH
SWA_KV_W = SWA_KV_HEADS * SWA_DH
WINDOW = 128
N_GROUPS = 4
EXPERTS_PER_GROUP = 4
N_EXPERTS = N_GROUPS * EXPERTS_PER_GROUP
D_EXPERT = 256
N_MOD = 6
EPS = 1e-6
IN_WIDTHS = (GLA_KEY_W, GLA_KEY_W, GLA_VAL_W, GLA_VAL_W, GLA_GATE_RANK,
             SWA_W, SWA_KV_W, SWA_KV_W, D_MODEL, D_MODEL)

LANES = 128
LOG2_E = 1.4426950408889634
LN_2 = 0.6931471805599453
VMEM_LIMIT = 56 * 1024 * 1024

TM_PROJ = 256
TS_GLA = 256
TQ_SWA = 256
TM_MERGE = 512
TM_MOE = 512
ROW_PACK = 16
MOE_PAD = 128
MOE_BM = 144
ROUTE_STRIDE = 8
WEIGHT_CHUNK = 2
TN_MOD = 512


def _params(sem):
    return pltpu.CompilerParams(dimension_semantics=sem, vmem_limit_bytes=VMEM_LIMIT)


def _split(a):
    hi = a.astype(BF16)
    lo = (a - hi.astype(F32)).astype(BF16)
    return hi, lo


def _dot(a, b):
    return jnp.dot(a, b, preferred_element_type=F32)


def _dot_nt(a, b):
    return lax.dot_general(a, b, (((1,), (1,)), ((), ())), preferred_element_type=F32)


def _dot_tn(a, b):
    return lax.dot_general(a, b, (((0,), (0,)), ((), ())), preferred_element_type=F32)


def _dot3(a, b):
    a_hi, a_lo = _split(a)
    b_hi, b_lo = _split(b)
    return _dot(a_hi, b_hi) + _dot(a_hi, b_lo) + _dot(a_lo, b_hi)


def _sigmoid(x):
    return 1.0 / (1.0 + jnp.exp(-x))


def _silu(x):
    return x * _sigmoid(x)


def _rms(x, g):
    return x * lax.rsqrt(jnp.mean(x * x, axis=-1, keepdims=True) + EPS) * g


def _mod_kernel(c_ref, w_ref, b_ref, o_ref):
    o_ref[...] = _dot3(_silu(c_ref[...]), w_ref[...]) + b_ref[...]


def _mod_call(c, w_ada, b_ada):
    bsz = c.shape[0]
    n = w_ada.shape[1]
    return pl.pallas_call(
        _mod_kernel,
        grid=(n // TN_MOD,),
        in_specs=[pl.BlockSpec((bsz, D_MODEL), lambda j: (0, 0)),
                  pl.BlockSpec((D_MODEL, TN_MOD), lambda j: (0, j)),
                  pl.BlockSpec((1, TN_MOD), lambda j: (0, j))],
        out_specs=pl.BlockSpec((bsz, TN_MOD), lambda j: (0, j)),
        out_shape=jax.ShapeDtypeStruct((bsz, n), F32),
        compiler_params=_params(("parallel",)),
        name="mod",
    )(c, w_ada, b_ada.reshape(1, n))


GATE_SLOT = 4


def _inproj_kernel(x_ref, g_ref, sc_ref, sh_ref, w2_ref, bgk_ref, tri_ref, *refs):
    n = len(refs) // 2
    w_refs, o_refs = refs[:n], refs[n:]
    h = _rms(x_ref[...], g_ref[...]) * (1.0 + sc_ref[...]) + sh_ref[...]
    hb = h.astype(BF16)
    z = _dot(hb, w_refs[GATE_SLOT][...])
    gz = _dot(z.astype(BF16), w2_ref[...]) + bgk_ref[...]
    soft = jnp.log2(1.0 + jnp.exp2(jnp.abs(gz) * (-LOG2_E)))
    la_hi, la_lo = _split(jnp.minimum(gz, 0.0) * (1.0 / GLA_TAU) - soft * (LN_2 / GLA_TAU))
    for slot, (w_ref, o_ref) in enumerate(zip(w_refs, o_refs)):
        if slot == GATE_SLOT:
            o_ref[...] = _dot(tri_ref[...], la_hi) + _dot(tri_ref[...], la_lo)
        else:
            o_ref[...] = _dot(hb, w_ref[...]).astype(o_ref.dtype)


def _inproj_call(x2, g1, sc1, sh1, w2p, bgk, weights, out_widths, out_dtypes, seq):
    t = x2.shape[0]
    per_b = seq // TM_PROJ
    vec = pl.BlockSpec((None, 1, D_MODEL), lambda i: (i // per_b, 0, 0))
    full = lambda a: pl.BlockSpec(a.shape, lambda i: (0, 0))
    idx = jnp.arange(TM_PROJ)
    tri = ((idx[:, None] // GLA_CHUNK == idx[None, :] // GLA_CHUNK) & (idx[None, :] <= idx[:, None])).astype(BF16)
    in_specs = [pl.BlockSpec((TM_PROJ, D_MODEL), lambda i: (i, 0)),
                pl.BlockSpec((1, D_MODEL), lambda i: (0, 0)), vec, vec, full(w2p), full(bgk), full(tri)]
    in_specs += [full(w) for w in weights]
    out_specs = [pl.BlockSpec((TM_PROJ, w), lambda i: (i, 0)) for w in out_widths]
    out_shape = [jax.ShapeDtypeStruct((t, w), dt) for w, dt in zip(out_widths, out_dtypes)]
    return pl.pallas_call(
        _inproj_kernel,
        grid=(t // TM_PROJ,),
        in_specs=in_specs,
        out_specs=out_specs,
        out_shape=out_shape,
        compiler_params=_params(("parallel",)),
        name="inproj",
    )(x2, g1, sc1, sh1, w2p, bgk, tri, *weights)


def _gla_kernel(q_ref, k_ref, v_ref, r_ref, bc_ref, gn_ref, o_ref, st_ref):
    ts = q_ref.shape[0]

    @pl.when(pl.program_id(1) == 0)
    def _():
        st_ref[...] = jnp.zeros_like(st_ref)

    nc = ts // GLA_CHUNK
    chunk = lambda c: slice(c * GLA_CHUNK, (c + 1) * GLA_CHUNK)
    bc = bc_ref[...]
    btot = [bc[(c + 1) * GLA_CHUNK - 1:(c + 1) * GLA_CHUNK, :] for c in range(nc)]
    pre = [jnp.zeros_like(btot[0])]
    for c in range(nc):
        pre.append(pre[c] + btot[c])
    rows = lambda vecs: jnp.concatenate([jnp.broadcast_to(v, (GLA_CHUNK, GLA_KEY_W)) for v in vecs], axis=0)

    qf = q_ref[...].astype(F32) * (GLA_DK ** -0.5) * jnp.exp(bc)
    kf = k_ref[...].astype(F32)
    q_dec = qf.astype(BF16)
    q_in = (qf * rows([jnp.exp(pre[c]) for c in range(nc)])).astype(BF16)
    k_inv = (kf * jnp.exp(-bc)).astype(BF16)
    k_end = kf * jnp.exp(rows(btot) - bc)
    k_tile = (k_end * rows([jnp.exp(pre[nc] - pre[c + 1]) for c in range(nc)])).astype(BF16)
    k_cross = {(c, c2): (k_end[chunk(c2)] * jnp.exp(pre[c] - pre[c2 + 1])).astype(BF16)
               for c in range(nc) for c2 in range(c)}

    row = lax.broadcasted_iota(jnp.int32, (ts, ts), 0)
    col = lax.broadcasted_iota(jnp.int32, (ts, ts), 1)
    causal = col <= row
    dec_tile = jnp.exp(pre[nc])
    gn = gn_ref[...]
    for h in range(GLA_HEADS):
        ks = slice(h * GLA_DK, (h + 1) * GLA_DK)
        vs = slice(h * GLA_DV, (h + 1) * GLA_DV)
        att = []
        for c in range(nc):
            keys = [k_cross[(c, c2)][:, ks] for c2 in range(c)] + [k_inv[chunk(c), ks]]
            keys += [jnp.zeros((GLA_CHUNK, GLA_DK), BF16)] * (nc - 1 - c)
            att.append(_dot_nt(q_dec[chunk(c), ks], jnp.concatenate(keys, axis=0)))
        att = jnp.where(causal, jnp.concatenate(att, axis=0), 0.0).astype(BF16)
        vh = v_ref[:, vs]
        st = st_ref[h]
        o = _dot(att, vh) + _dot(q_in[:, ks], st.astype(BF16))
        dec_col = jnp.transpose(jnp.broadcast_to(dec_tile[:, ks], (8, GLA_DK)))[:, 0:1]
        st_ref[h] = st * dec_col + _dot_tn(k_tile[:, ks], vh)
        o = o * lax.rsqrt(jnp.mean(o * o, axis=-1, keepdims=True) + EPS) * gn
        o_ref[:, vs] = (o * _silu(r_ref[:, vs].astype(F32))).astype(o_ref.dtype)


def _gla_call(qa, ka, va, ra, bc, gn, bsz, seq):
    ns = seq // TS_GLA
    tok = lambda w: pl.BlockSpec((TS_GLA, w), lambda b, s: (b * ns + s, 0))
    return pl.pallas_call(
        _gla_kernel,
        grid=(bsz, ns),
        in_specs=[tok(GLA_KEY_W), tok(GLA_KEY_W), tok(GLA_VAL_W), tok(GLA_VAL_W), tok(GLA_KEY_W),
                  pl.BlockSpec(gn.shape, lambda b, s: (0, 0))],
        out_specs=tok(GLA_VAL_W),
        out_shape=jax.ShapeDtypeStruct((bsz * seq, GLA_VAL_W), BF16),
        scratch_shapes=[pltpu.VMEM((GLA_HEADS, GLA_DK, GLA_DV), F32)],
        compiler_params=_params(("parallel", "arbitrary")),
        name="gla",
    )(qa, ka, va, ra, bc, gn)


def _swa_kernel(q_ref, kc_ref, kp_ref, vc_ref, vp_ref, fill_ref, o_ref):
    w = WINDOW
    tq = q_ref.shape[0]
    first = pl.program_id(1) == 0
    zero_prev = lambda a: jnp.where(first, jnp.zeros_like(a), a)
    k_all = jnp.concatenate([zero_prev(kp_ref[...]), kc_ref[...]], axis=0)
    v_all = jnp.concatenate([zero_prev(vp_ref[...]), vc_ref[...]], axis=0)
    lane = lax.broadcasted_iota(jnp.int32, (w + tq, LANES), 1)
    lo_half = lane < SWA_DH
    k_sw = pltpu.roll(k_all, SWA_DH, 1)
    v_sw = pltpu.roll(v_all, SWA_DH, 1)
    k2 = (jnp.where(lo_half, k_all, k_sw), jnp.where(lo_half, k_sw, k_all))
    v2 = (jnp.where(lo_half, v_all, v_sw), jnp.where(lo_half, v_sw, v_all))

    qi = lax.broadcasted_iota(jnp.int32, (w, 2 * w), 0) + w
    kj = lax.broadcasted_iota(jnp.int32, (w, 2 * w), 1)
    rel = qi - kj
    band = (rel >= 0) & (rel < w)
    band_first = band & (jnp.logical_not(first) | (kj >= w))
    q_lo = lax.broadcasted_iota(jnp.int32, (w, LANES), 1) < SWA_DH
    row0 = lax.broadcasted_iota(jnp.int32, (2 * w, LANES), 0) == 0
    ones = jnp.ones((2 * w, LANES), BF16)
    scale = jnp.asarray(SWA_DH ** -0.5, BF16)

    for blk in range(tq // w):
        valid = band_first if blk == 0 else band
        qs = slice(blk * w, (blk + 1) * w)
        ws = slice(blk * w, blk * w + 2 * w)
        for hk in range(SWA_KV_HEADS):
            parts = []
            for p in range(SWA_GROUP // 2):
                cs = slice((hk * (SWA_GROUP // 2) + p) * LANES, (hk * (SWA_GROUP // 2) + p + 1) * LANES)
                qp = q_ref[qs, cs] * scale
                parts += [jnp.where(q_lo, qp, jnp.zeros_like(qp)), jnp.where(q_lo, jnp.zeros_like(qp), qp)]
            s = _dot_nt(jnp.concatenate(parts, axis=0), k2[hk][ws])
            es = []
            for r in range(SWA_GROUP):
                head = hk * SWA_GROUP + r
                sh = jnp.where(valid, s[r * w:(r + 1) * w], fill_ref[head:head + 1, :])
                es.append(jnp.exp(sh - jnp.max(sh, axis=-1, keepdims=True)).astype(BF16))
            vv = jnp.where(row0, jnp.zeros_like(ones), v2[hk][ws])
            o = _dot(jnp.concatenate(es, axis=0), jnp.concatenate([vv, ones], axis=1))
            for p in range(SWA_GROUP // 2):
                cs = slice((hk * (SWA_GROUP // 2) + p) * LANES, (hk * (SWA_GROUP // 2) + p + 1) * LANES)
                oe = o[(2 * p) * w:(2 * p + 1) * w]
                oo = o[(2 * p + 1) * w:(2 * p + 2) * w]
                num = jnp.where(q_lo, oe[:, :LANES], oo[:, :LANES])
                den = jnp.where(q_lo, oe[:, LANES:], oo[:, LANES:])
                o_ref[qs, cs] = (num / den).astype(o_ref.dtype)


def _swa_call(qb, kb, vb, fill, bsz, seq):
    nt = seq // TQ_SWA
    per = TQ_SWA // WINDOW
    cur = lambda w: pl.BlockSpec((TQ_SWA, w), lambda b, n: (b * nt + n, 0))
    prev = lambda w: pl.BlockSpec((WINDOW, w), lambda b, n: (jnp.maximum((b * nt + n) * per - 1, 0), 0))
    return pl.pallas_call(
        _swa_kernel,
        grid=(bsz, nt),
        in_specs=[cur(SWA_W), cur(SWA_KV_W), prev(SWA_KV_W), cur(SWA_KV_W), prev(SWA_KV_W),
                  pl.BlockSpec(fill.shape, lambda b, n: (0, 0))],
        out_specs=cur(SWA_W),
        out_shape=jax.ShapeDtypeStruct((bsz * seq, SWA_W), BF16),
        compiler_params=_params(("parallel", "arbitrary")),
        name="swa",
    )(qb, kb, kb, vb, vb, fill)


def _merge_kernel(x_ref, oa_ref, ob_ref, ga_ref, gb_ref, gt_ref, wo_ref, o_ref):
    merged = (_sigmoid(ga_ref[...].astype(F32)) * oa_ref[...].astype(F32)
              + _sigmoid(gb_ref[...].astype(F32)) * ob_ref[...].astype(F32))
    o_ref[...] = x_ref[...] + gt_ref[...] * _dot(merged.astype(BF16), wo_ref[...])


def _merge_call(x2, oa, ob, ga, gb, gt1, wo, seq):
    t = x2.shape[0]
    per_b = seq // TM_MERGE
    tok = pl.BlockSpec((TM_MERGE, D_MODEL), lambda i: (i, 0))
    return pl.pallas_call(
        _merge_kernel,
        grid=(t // TM_MERGE,),
        in_specs=[tok, tok, tok, tok, tok,
                  pl.BlockSpec((None, 1, D_MODEL), lambda i: (i // per_b, 0, 0)),
                  pl.BlockSpec((D_MODEL, D_MODEL), lambda i: (0, 0))],
        out_specs=tok,
        out_shape=jax.ShapeDtypeStruct((t, D_MODEL), F32),
        compiler_params=_params(("parallel",)),
        name="merge",
    )(x2, oa, ob, ga, gb, gt1, wo)


def _first_of(vals, target):
    idx = jnp.full(target.shape, len(vals) - 1, jnp.int32)
    for j in range(len(vals) - 2, -1, -1):
        idx = jnp.where(vals[j] == target, j, idx)
    return idx


def _rows_to_cols(a):
    return jnp.concatenate([jnp.transpose(a[:, i * LANES:(i + 1) * LANES]) for i in range(a.shape[1] // LANES)],
                           axis=0)


def _moe_kernel(x_ref, g2_ref, sc_ref, sh_ref, gt_ref, wrt_ref, brt_ref, upper_ref, wg_ref, wu_ref, wd_ref,
                gf_ref, o_ref, hs_ref, cws_ref, ys_ref):
    x1 = x_ref[...]
    h = _rms(x1, g2_ref[...]) * (1.0 + sc_ref[...]) + sh_ref[...]
    hb = h.astype(BF16)
    tm = x1.shape[0]
    neg = -jnp.inf

    lt = _dot_nt(wrt_ref[...], hb) + brt_ref[...]
    lrow = lambda k: lt[ROUTE_STRIDE * k:ROUTE_STRIDE * k + 1, :]
    lg = [lrow(g) for g in range(N_GROUPS)]
    gmax = functools.reduce(jnp.maximum, lg)
    g_w = 1.0 / functools.reduce(jnp.add, [jnp.exp(v - gmax) for v in lg])
    g_idx = _first_of(lg, gmax)
    le = []
    for j in range(EXPERTS_PER_GROUP):
        v = lrow(N_GROUPS + (N_GROUPS - 1) * EXPERTS_PER_GROUP + j)
        for g in range(N_GROUPS - 2, -1, -1):
            v = jnp.where(g_idx == g, lrow(N_GROUPS + g * EXPERTS_PER_GROUP + j), v)
        le.append(v)
    v1 = functools.reduce(jnp.maximum, le)
    i1 = _first_of(le, v1)
    le2 = [jnp.where(i1 == j, neg, le[j]) for j in range(EXPERTS_PER_GROUP)]
    v2 = functools.reduce(jnp.maximum, le2)
    i2 = _first_of(le2, v2)
    e2 = jnp.exp(v2 - v1)
    w1 = g_w / (1.0 + e2)
    w2 = g_w * e2 / (1.0 + e2)

    sub = lax.broadcasted_iota(jnp.int32, (ROW_PACK, tm), 0)
    onehot = (sub == g_idx).astype(BF16)
    rank = _dot(onehot, upper_ref[...])
    pcs, offs, run = [], [], 0
    pos = jnp.zeros((1, tm), F32)
    for g in range(N_GROUPS):
        mine = g_idx == g
        cnt = jnp.sum(mine.astype(F32), axis=1, keepdims=True)[0, 0].astype(jnp.int32)
        pcs.append(((cnt + (ROW_PACK - 1)) // ROW_PACK) * ROW_PACK)
        offs.append(run)
        pos = pos + jnp.where(mine, rank[g:g + 1, :] + jnp.asarray(run, F32), 0.0)
        run = run + pcs[g]
    rows = tm + MOE_PAD
    perm = (lax.broadcasted_iota(jnp.int32, (rows, tm), 0) == pos.astype(jnp.int32)).astype(BF16)
    hs_ref[0:rows, :] = _dot(perm, hb).astype(BF16)
    c_hi, c_lo = _split(jnp.where(sub == i1, w1, 0.0) + jnp.where(sub == i2, w2, 0.0))
    cws_ref[0:rows, :] = _rows_to_cols(_dot_nt(c_hi, perm) + _dot_nt(c_lo, perm))
    pos_col = _rows_to_cols(jnp.broadcast_to(pos, (8, tm)))[:, 0:1].astype(jnp.int32)
    sel_t = (lax.broadcasted_iota(jnp.int32, (tm, rows), 1) == pos_col).astype(BF16)
    hs_ref[rows:, :] = jnp.zeros((MOE_BM, D_MODEL), BF16)
    cws_ref[rows:, :] = jnp.zeros((MOE_BM, ROW_PACK), F32)
    ys_ref[...] = jnp.zeros_like(ys_ref)

    for g in range(N_GROUPS):
        def block(k, carry, g=g):
            start = pl.multiple_of(offs[g] + k * MOE_BM, ROW_PACK)
            hblk = hs_ref[pl.ds(start, MOE_BM), :]
            inside = (start + lax.broadcasted_iota(jnp.int32, (MOE_BM, ROW_PACK), 0)) < offs[g] + pcs[g]
            cw = jnp.where(inside, cws_ref[pl.ds(start, MOE_BM), :], 0.0)
            experts = [g * EXPERTS_PER_GROUP + j for j in range(EXPERTS_PER_GROUP)]
            gates = [_dot(hblk, wg_ref[e]) for e in experts]
            ups = [_dot(hblk, wu_ref[e]) for e in experts]
            hids = [(_silu(gates[j]) * ups[j] * cw[:, j:j + 1]).astype(BF16) for j in range(EXPERTS_PER_GROUP)]
            y = _dot(hids[0], wd_ref[experts[0]])
            for j in range(1, EXPERTS_PER_GROUP):
                y = y + _dot(hids[j], wd_ref[experts[j]])
            ys_ref[pl.ds(start, MOE_BM), :] += y
            return carry
        lax.fori_loop(0, (pcs[g] + (MOE_BM - 1)) // MOE_BM, block, 0)

    y = _dot(sel_t, ys_ref[0:rows, :].astype(BF16))
    o_ref[...] = _rms(x1 + gt_ref[...] * y, gf_ref[...])


def _load_bf16(src, dst, stage, sem):
    n = src.shape[0] // WEIGHT_CHUNK
    copy = lambda i: pltpu.make_async_copy(src.at[pl.ds(i * WEIGHT_CHUNK, WEIGHT_CHUNK)], stage.at[i % 2],
                                           sem.at[i % 2])
    copy(0).start()
    for i in range(n):
        if i + 1 < n:
            copy(i + 1).start()
        copy(i).wait()
        dst[pl.ds(i * WEIGHT_CHUNK, WEIGHT_CHUNK)] = stage[i % 2].astype(BF16)


def _moe_body(x_ref, g2_ref, sc_ref, sh_ref, gt_ref, wrt_ref, brt_ref, upper_ref, wg_hbm, wu_hbm, wd_hbm, gf_ref,
              o_ref, wg_ref, wu_ref, wd_ref, hs_ref, cws_ref, ys_ref, stage_in, stage_out, sem):
    @pl.when(pl.program_id(0) == 0)
    def _():
        _load_bf16(wg_hbm, wg_ref, stage_in, sem)
        _load_bf16(wu_hbm, wu_ref, stage_in, sem)
        _load_bf16(wd_hbm, wd_ref, stage_out, sem)

    _moe_kernel(x_ref, g2_ref, sc_ref, sh_ref, gt_ref, wrt_ref, brt_ref, upper_ref, wg_ref, wu_ref, wd_ref,
                gf_ref, o_ref, hs_ref, cws_ref, ys_ref)


def _moe_call(x1, g2, sc2, sh2, gt2, wrt, brt, wg, wu, wd, gf, seq):
    t = x1.shape[0]
    per_b = seq // TM_MOE
    tok = pl.BlockSpec((TM_MOE, D_MODEL), lambda i: (i, 0))
    vec = pl.BlockSpec((None, 1, D_MODEL), lambda i: (i // per_b, 0, 0))
    row = pl.BlockSpec((1, D_MODEL), lambda i: (0, 0))
    full = lambda a: pl.BlockSpec(a.shape, lambda i: (0,) * a.ndim)
    hbm = pl.BlockSpec(memory_space=pl.ANY)
    srows = TM_MOE + MOE_PAD + MOE_BM
    upper = jnp.triu(jnp.ones((TM_MOE, TM_MOE), BF16), k=1)
    return pl.pallas_call(
        _moe_body,
        grid=(t // TM_MOE,),
        in_specs=[tok, row, vec, vec, vec, full(wrt), full(brt), full(upper), hbm, hbm, hbm, row],
        out_specs=tok,
        out_shape=jax.ShapeDtypeStruct((t, D_MODEL), F32),
        scratch_shapes=[pltpu.VMEM(wg.shape, BF16), pltpu.VMEM(wu.shape, BF16), pltpu.VMEM(wd.shape, BF16),
                        pltpu.VMEM((srows, D_MODEL), BF16), pltpu.VMEM((srows, ROW_PACK), F32),
                        pltpu.VMEM((srows, D_MODEL), F32),
                        pltpu.VMEM((2, WEIGHT_CHUNK) + wg.shape[1:], F32),
                        pltpu.VMEM((2, WEIGHT_CHUNK) + wd.shape[1:], F32),
                        pltpu.SemaphoreType.DMA((2,))],
        compiler_params=_params(("arbitrary",)),
        name="moe",
    )(x1, g2, sc2, sh2, gt2, wrt, brt, upper, wg, wu, wd, gf)


def kernel(x, c, w_ada, b_ada, norm1_g, w_in, w_gk2, b_gk, gla_norm_g, sink, w_o, norm2_g,
           w_group, b_group, w_router, b_router, w_gate, w_up, w_down, norm_f_g):
    bsz, seq, d = x.shape
    depth = w_ada.shape[0]
    assert depth == 1, "the final norm is fused into the (single) layer's MoE kernel"
    x2 = x.reshape(bsz * seq, d)
    for l in range(depth):
        mod = _mod_call(c, w_ada[l], b_ada[l])
        sh1, sc1, gt1, sh2, sc2, gt2 = [m.reshape(bsz, 1, d) for m in jnp.split(mod, N_MOD, axis=-1)]

        w_parts = _split_cols(w_in[l])
        w_parts[GATE_SLOT] = jnp.pad(w_parts[GATE_SLOT], ((0, 0), (0, LANES - GLA_GATE_RANK)))
        weights = [w.astype(BF16) for w in w_parts]
        out_widths = [GLA_KEY_W if s == GATE_SLOT else w.shape[1] for s, w in enumerate(weights)]
        out_dtypes = [F32 if s == GATE_SLOT else BF16 for s in range(len(weights))]
        w2p = jnp.pad(w_gk2[l], ((0, LANES - GLA_GATE_RANK), (0, 0))).astype(BF16)
        qa, ka, va, ra, bc, qb, kb, vb, ga, gb = _inproj_call(
            x2, norm1_g[l].reshape(1, d), sc1, sh1, w2p, b_gk[l].reshape(1, GLA_KEY_W),
            weights, out_widths, out_dtypes, seq)

        oa = _gla_call(qa, ka, va, ra, bc, gla_norm_g[l].reshape(1, GLA_DV), bsz, seq)
        fill = jnp.full((SWA_HEADS, 2 * WINDOW), -jnp.inf, F32).at[:, 0].set(sink[l])
        ob = _swa_call(qb, kb, vb, fill, bsz, seq)
        x2 = _merge_call(x2, oa, ob, ga, gb, gt1, w_o[l].astype(BF16), seq)

        n_logit = N_GROUPS + N_EXPERTS
        wrt = jnp.zeros((n_logit, ROUTE_STRIDE, d), F32).at[:, 0, :].set(
            jnp.concatenate([w_group[l], w_router[l]], axis=1).T).reshape(n_logit * ROUTE_STRIDE, d).astype(BF16)
        brt = jnp.zeros((n_logit, ROUTE_STRIDE), F32).at[:, 0].set(
            jnp.concatenate([b_group[l], b_router[l]])).reshape(n_logit * ROUTE_STRIDE, 1)
        x2 = _moe_call(x2, norm2_g[l].reshape(1, d), sc2, sh2, gt2, wrt, brt,
                       w_gate[l], w_up[l], w_down[l], norm_f_g.reshape(1, d), seq)
    return x2.reshape(bsz, seq, d)


def _split_cols(w):
    parts, start = [], 0
    for width in IN_WIDTHS:
        parts.append(w[:, start:start + width])
        start += width
    return parts
```

```python
import functools

import jax
import jax.numpy as jnp
from jax import lax
from jax.experimental import pallas as pl
from jax.experimental.pallas import tpu as pltpu

F32 = jnp.float32
BF16 = jnp.bfloat16

D_MODEL = 1024
GLA_HEADS = 4
GLA_DK = 128
GLA_DV = 256
GLA_KEY_W = GLA_HEADS * GLA_DK
GLA_VAL_W = GLA_HEADS * GLA_DV
GLA_GATE_RANK = 16
GLA_TAU = 16.0
GLA_CHUNK = 64
SWA_HEADS = 16
SWA_KV_HEADS = 2
SWA_GROUP = SWA_HEADS // SWA_KV_HEADS
SWA_DH = 64
SWA_W = SWA_HEADS * SWA_DH
SWA_KV_W = SWA_KV_HEADS * SWA_DH
WINDOW = 128
N_GROUPS = 4
EXPERTS_PER_GROUP = 4
N_EXPERTS = N_GROUPS * EXPERTS_PER_GROUP
D_EXPERT = 256
N_MOD = 6
EPS = 1e-6
IN_WIDTHS = (GLA_KEY_W, GLA_KEY_W, GLA_VAL_W, GLA_VAL_W, GLA_GATE_RANK,
             SWA_W, SWA_KV_W, SWA_KV_W, D_MODEL, D_MODEL)

LANES = 128
LOG2_E = 1.4426950408889634
LN_2 = 0.6931471805599453
VMEM_LIMIT = 56 * 1024 * 1024

TM_MIX = 512
MIX_SUB = 256
TM_PROJ = 256
TS_GLA = 256
TQ_SWA = 256
TM_MERGE = 512
TM_MOE = 512
ROW_PACK = 16
MOE_PAD = 128
MOE_BM = 144
ROUTE_STRIDE = 8
WEIGHT_CHUNK = 2
TN_MOD = 512


def _params(sem):
    return pltpu.CompilerParams(dimension_semantics=sem, vmem_limit_bytes=VMEM_LIMIT)


def _split(a):
    hi = a.astype(BF16)
    lo = (a - hi.astype(F32)).astype(BF16)
    return hi, lo


def _dot(a, b):
    return jnp.dot(a, b, preferred_element_type=F32)


def _dot_nt(a, b):
    return lax.dot_general(a, b, (((1,), (1,)), ((), ())), preferred_element_type=F32)


def _dot_tn(a, b):
    return lax.dot_general(a, b, (((0,), (0,)), ((), ())), preferred_element_type=F32)


def _dot3(a, b):
    a_hi, a_lo = _split(a)
    b_hi, b_lo = _split(b)
    return _dot(a_hi, b_hi) + _dot(a_hi, b_lo) + _dot(a_lo, b_hi)


def _sigmoid(x):
    return 1.0 / (1.0 + jnp.exp(-x))


def _silu(x):
    return x * _sigmoid(x)


def _rms(x, g):
    return x * lax.rsqrt(jnp.mean(x * x, axis=-1, keepdims=True) + EPS) * g


def _mod_kernel(c_ref, w_ref, b_ref, o_ref):
    o_ref[...] = _dot3(_silu(c_ref[...]), w_ref[...]) + b_ref[...]


def _mod_call(c, w_ada, b_ada):
    bsz = c.shape[0]
    n = w_ada.shape[1]
    return pl.pallas_call(
        _mod_kernel,
        grid=(n // TN_MOD,),
        in_specs=[pl.BlockSpec((bsz, D_MODEL), lambda j: (0, 0)),
                  pl.BlockSpec((D_MODEL, TN_MOD), lambda j: (0, j)),
                  pl.BlockSpec((1, TN_MOD), lambda j: (0, j))],
        out_specs=pl.BlockSpec((bsz, TN_MOD), lambda j: (0, j)),
        out_shape=jax.ShapeDtypeStruct((bsz, n), F32),
        compiler_params=_params(("parallel",)),
        name="mod",
    )(c, w_ada, b_ada.reshape(1, n))


GATE_SLOT = 4


def _inproj_kernel(x_ref, g_ref, sc_ref, sh_ref, w2_ref, bgk_ref, tri_ref, *refs):
    n = len(refs) // 2
    w_refs, o_refs = refs[:n], refs[n:]
    h = _rms(x_ref[...], g_ref[...]) * (1.0 + sc_ref[...]) + sh_ref[...]
    hb = h.astype(BF16)
    z = _dot(hb, w_refs[GATE_SLOT][...])
    gz = _dot(z.astype(BF16), w2_ref[...]) + bgk_ref[...]
    soft = jnp.log2(1.0 + jnp.exp2(jnp.abs(gz) * (-LOG2_E)))
    la_hi, la_lo = _split(jnp.minimum(gz, 0.0) * (1.0 / GLA_TAU) - soft * (LN_2 / GLA_TAU))
    for slot, (w_ref, o_ref) in enumerate(zip(w_refs, o_refs)):
        if slot == GATE_SLOT:
            o_ref[...] = _dot(tri_ref[...], la_hi) + _dot(tri_ref[...], la_lo)
        else:
            o_ref[...] = _dot(hb, w_ref[...]).astype(o_ref.dtype)


def _inproj_call(x2, g1, sc1, sh1, w2p, bgk, weights, out_widths, out_dtypes, seq):
    t = x2.shape[0]
    per_b = seq // TM_PROJ
    vec = pl.BlockSpec((None, 1, D_MODEL), lambda i: (i // per_b, 0, 0))
    full = lambda a: pl.BlockSpec(a.shape, lambda i: (0, 0))
    idx = jnp.arange(TM_PROJ)
    tri = ((idx[:, None] // GLA_CHUNK == idx[None, :] // GLA_CHUNK) & (idx[None, :] <= idx[:, None])).astype(BF16)
    in_specs = [pl.BlockSpec((TM_PROJ, D_MODEL), lambda i: (i, 0)),
                pl.BlockSpec((1, D_MODEL), lambda i: (0, 0)), vec, vec, full(w2p), full(bgk), full(tri)]
    in_specs += [full(w) for w in weights]
    out_specs = [pl.BlockSpec((TM_PROJ, w), lambda i: (i, 0)) for w in out_widths]
    out_shape = [jax.ShapeDtypeStruct((t, w), dt) for w, dt in zip(out_widths, out_dtypes)]
    return pl.pallas_call(
        _inproj_kernel,
        grid=(t // TM_PROJ,),
        in_specs=in_specs,
        out_specs=out_specs,
        out_shape=out_shape,
        compiler_params=_params(("parallel",)),
        name="inproj",
    )(x2, g1, sc1, sh1, w2p, bgk, tri, *weights)


def _gla_kernel(q_ref, k_ref, v_ref, r_ref, bc_ref, gn_ref, o_ref, st_ref):
    ts = q_ref.shape[0]

    @pl.when(pl.program_id(1) == 0)
    def _():
        st_ref[...] = jnp.zeros_like(st_ref)

    nc = ts // GLA_CHUNK
    chunk = lambda c: slice(c * GLA_CHUNK, (c + 1) * GLA_CHUNK)
    bc = bc_ref[...]
    btot = [bc[(c + 1) * GLA_CHUNK - 1:(c + 1) * GLA_CHUNK, :] for c in range(nc)]
    pre = [jnp.zeros_like(btot[0])]
    for c in range(nc):
        pre.append(pre[c] + btot[c])
    rows = lambda vecs: jnp.concatenate([jnp.broadcast_to(v, (GLA_CHUNK, GLA_KEY_W)) for v in vecs], axis=0)

    qf = q_ref[...].astype(F32) * (GLA_DK ** -0.5) * jnp.exp(bc)
    kf = k_ref[...].astype(F32)
    q_dec = qf.astype(BF16)
    q_in = (qf * rows([jnp.exp(pre[c]) for c in range(nc)])).astype(BF16)
    k_inv = (kf * jnp.exp(-bc)).astype(BF16)
    k_end = kf * jnp.exp(rows(btot) - bc)
    k_tile = (k_end * rows([jnp.exp(pre[nc] - pre[c + 1]) for c in range(nc)])).astype(BF16)
    k_cross = {(c, c2): (k_end[chunk(c2)] * jnp.exp(pre[c] - pre[c2 + 1])).astype(BF16)
               for c in range(nc) for c2 in range(c)}

    row = lax.broadcasted_iota(jnp.int32, (ts, ts), 0)
    col = lax.broadcasted_iota(jnp.int32, (ts, ts), 1)
    causal = col <= row
    dec_tile = jnp.exp(pre[nc])
    gn = gn_ref[...]
    for h in range(GLA_HEADS):
        ks = slice(h * GLA_DK, (h + 1) * GLA_DK)
        vs = slice(h * GLA_DV, (h + 1) * GLA_DV)
        att = []
        for c in range(nc):
            keys = [k_cross[(c, c2)][:, ks] for c2 in range(c)] + [k_inv[chunk(c), ks]]
            keys += [jnp.zeros((GLA_CHUNK, GLA_DK), BF16)] * (nc - 1 - c)
            att.append(_dot_nt(q_dec[chunk(c), ks], jnp.concatenate(keys, axis=0)))
        att = jnp.where(causal, jnp.concatenate(att, axis=0), 0.0).astype(BF16)
        vh = v_ref[:, vs]
        st = st_ref[h]
        o = _dot(att, vh) + _dot(q_in[:, ks], st.astype(BF16))
        dec_col = jnp.transpose(jnp.broadcast_to(dec_tile[:, ks], (8, GLA_DK)))[:, 0:1]
        st_ref[h] = st * dec_col + _dot_tn(k_tile[:, ks], vh)
        o = o * lax.rsqrt(jnp.mean(o * o, axis=-1, keepdims=True) + EPS) * gn
        o_ref[:, vs] = (o * _silu(r_ref[:, vs].astype(F32))).astype(o_ref.dtype)


def _gla_call(qa, ka, va, ra, bc, gn, bsz, seq):
    ns = seq // TS_GLA
    tok = lambda w: pl.BlockSpec((TS_GLA, w), lambda b, s: (b * ns + s, 0))
    return pl.pallas_call(
        _gla_kernel,
        grid=(bsz, ns),
        in_specs=[tok(GLA_KEY_W), tok(GLA_KEY_W), tok(GLA_VAL_W), tok(GLA_VAL_W), tok(GLA_KEY_W),
                  pl.BlockSpec(gn.shape, lambda b, s: (0, 0))],
        out_specs=tok(GLA_VAL_W),
        out_shape=jax.ShapeDtypeStruct((bsz * seq, GLA_VAL_W), BF16),
        scratch_shapes=[pltpu.VMEM((GLA_HEADS, GLA_DK, GLA_DV), F32)],
        compiler_params=_params(("parallel", "arbitrary")),
        name="gla",
    )(qa, ka, va, ra, bc, gn)


def _swa_kernel(q_ref, kc_ref, kp_ref, vc_ref, vp_ref, fill_ref, o_ref):
    w = WINDOW
    tq = q_ref.shape[0]
    first = pl.program_id(1) == 0
    zero_prev = lambda a: jnp.where(first, jnp.zeros_like(a), a)
    k_all = jnp.concatenate([zero_prev(kp_ref[...]), kc_ref[...]], axis=0)
    v_all = jnp.concatenate([zero_prev(vp_ref[...]), vc_ref[...]], axis=0)
    lane = lax.broadcasted_iota(jnp.int32, (w + tq, LANES), 1)
    lo_half = lane < SWA_DH
    k_sw = pltpu.roll(k_all, SWA_DH, 1)
    v_sw = pltpu.roll(v_all, SWA_DH, 1)
    k2 = (jnp.where(lo_half, k_all, k_sw), jnp.where(lo_half, k_sw, k_all))
    v2 = (jnp.where(lo_half, v_all, v_sw), jnp.where(lo_half, v_sw, v_all))

    qi = lax.broadcasted_iota(jnp.int32, (w, 2 * w), 0) + w
    kj = lax.broadcasted_iota(jnp.int32, (w, 2 * w), 1)
    rel = qi - kj
    band = (rel >= 0) & (rel < w)
    band_first = band & (jnp.logical_not(first) | (kj >= w))
    q_lo = lax.broadcasted_iota(jnp.int32, (w, LANES), 1) < SWA_DH
    row0 = lax.broadcasted_iota(jnp.int32, (2 * w, LANES), 0) == 0
    ones = jnp.ones((2 * w, LANES), BF16)
    scale = jnp.asarray(SWA_DH ** -0.5, BF16)

    for blk in range(tq // w):
        valid = band_first if blk == 0 else band
        qs = slice(blk * w, (blk + 1) * w)
        ws = slice(blk * w, blk * w + 2 * w)
        for hk in range(SWA_KV_HEADS):
            parts = []
            for p in range(SWA_GROUP // 2):
                cs = slice((hk * (SWA_GROUP // 2) + p) * LANES, (hk * (SWA_GROUP // 2) + p + 1) * LANES)
                qp = q_ref[qs, cs] * scale
                parts += [jnp.where(q_lo, qp, jnp.zeros_like(qp)), jnp.where(q_lo, jnp.zeros_like(qp), qp)]
            s = _dot_nt(jnp.concatenate(parts, axis=0), k2[hk][ws])
            es = []
            for r in range(SWA_GROUP):
                head = hk * SWA_GROUP + r
                sh = jnp.where(valid, s[r * w:(r + 1) * w], fill_ref[head:head + 1, :])
                es.append(jnp.exp(sh - jnp.max(sh, axis=-1, keepdims=True)).astype(BF16))
            vv = jnp.where(row0, jnp.zeros_like(ones), v2[hk][ws])
            o = _dot(jnp.concatenate(es, axis=0), jnp.concatenate([vv, ones], axis=1))
            for p in range(SWA_GROUP // 2):
                cs = slice((hk * (SWA_GROUP // 2) + p) * LANES, (hk * (SWA_GROUP // 2) + p + 1) * LANES)
                oe = o[(2 * p) * w:(2 * p + 1) * w]
                oo = o[(2 * p + 1) * w:(2 * p + 2) * w]
                num = jnp.where(q_lo, oe[:, :LANES], oo[:, :LANES])
                den = jnp.where(q_lo, oe[:, LANES:], oo[:, LANES:])
                o_ref[qs, cs] = (num / den).astype(o_ref.dtype)


def _swa_call(qb, kb, vb, fill, bsz, seq):
    nt = seq // TQ_SWA
    per = TQ_SWA // WINDOW
    cur = lambda w: pl.BlockSpec((TQ_SWA, w), lambda b, n: (b * nt + n, 0))
    prev = lambda w: pl.BlockSpec((WINDOW, w), lambda b, n: (jnp.maximum((b * nt + n) * per - 1, 0), 0))
    return pl.pallas_call(
        _swa_kernel,
        grid=(bsz, nt),
        in_specs=[cur(SWA_W), cur(SWA_KV_W), prev(SWA_KV_W), cur(SWA_KV_W), prev(SWA_KV_W),
                  pl.BlockSpec(fill.shape, lambda b, n: (0, 0))],
        out_specs=cur(SWA_W),
        out_shape=jax.ShapeDtypeStruct((bsz * seq, SWA_W), BF16),
        compiler_params=_params(("parallel", "arbitrary")),
        name="swa",
    )(qb, kb, kb, vb, vb, fill)


def _merge_kernel(x_ref, oa_ref, ob_ref, ga_ref, gb_ref, gt_ref, wo_ref, o_ref):
    merged = (_sigmoid(ga_ref[...].astype(F32)) * oa_ref[...].astype(F32)
              + _sigmoid(gb_ref[...].astype(F32)) * ob_ref[...].astype(F32))
    o_ref[...] = x_ref[...] + gt_ref[...] * _dot(merged.astype(BF16), wo_ref[...])


def _merge_call(x2, oa, ob, ga, gb, gt1, wo, seq):
    t = x2.shape[0]
    per_b = seq // TM_MERGE
    tok = pl.BlockSpec((TM_MERGE, D_MODEL), lambda i: (i, 0))
    return pl.pallas_call(
        _merge_kernel,
        grid=(t // TM_MERGE,),
        in_specs=[tok, tok, tok, tok, tok,
                  pl.BlockSpec((None, 1, D_MODEL), lambda i: (i // per_b, 0, 0)),
                  pl.BlockSpec((D_MODEL, D_MODEL), lambda i: (0, 0))],
        out_specs=tok,
        out_shape=jax.ShapeDtypeStruct((t, D_MODEL), F32),
        compiler_params=_params(("parallel",)),
        name="merge",
    )(x2, oa, ob, ga, gb, gt1, wo)


QA, KA, VA, RA, GATE, QB, KB, VB, GA, GB, WO = range(11)
N_WEIGHTS = 11


def _mixer_kernel(x_ref, g1_ref, sc_ref, sh_ref, gt_ref, w2_ref, bgk_ref, tri_ref, gn_ref, fill_ref, *rest):
    w_hbm, o_ref, w = rest[:N_WEIGHTS], rest[N_WEIGHTS], rest[N_WEIGHTS + 1:2 * N_WEIGHTS + 1]
    st_ref, kprev_ref, vprev_ref, oa_ref, ob_ref = rest[2 * N_WEIGHTS + 1:]
    first = pl.program_id(1) == 0

    @pl.when(first & (pl.program_id(0) == 0))
    def _():
        for src, dst in zip(w_hbm, w):
            pltpu.sync_copy(src, dst)

    @pl.when(first)
    def _():
        st_ref[...] = jnp.zeros_like(st_ref)
        kprev_ref[...] = jnp.zeros_like(kprev_ref)
        vprev_ref[...] = jnp.zeros_like(vprev_ref)

    for sub in range(x_ref.shape[0] // MIX_SUB):
        _mixer_tile(sub * MIX_SUB, first if sub == 0 else False, x_ref, g1_ref, sc_ref, sh_ref, gt_ref, w2_ref,
                    bgk_ref, tri_ref, gn_ref, fill_ref, o_ref, w, st_ref, kprev_ref, vprev_ref, oa_ref, ob_ref)


def _mixer_tile(r0, first, x_ref, g1_ref, sc_ref, sh_ref, gt_ref, w2_ref, bgk_ref, tri_ref, gn_ref, fill_ref,
                o_ref, w, st_ref, kprev_ref, vprev_ref, oa_ref, ob_ref):
    ts = MIX_SUB
    x = x_ref[r0:r0 + ts, :]
    hb = (_rms(x, g1_ref[...]) * (1.0 + sc_ref[...]) + sh_ref[...]).astype(BF16)

    z = _dot(hb, w[GATE][...])
    gz = _dot(z.astype(BF16), w2_ref[...]) + bgk_ref[...]
    soft = jnp.log2(1.0 + jnp.exp2(jnp.abs(gz) * (-LOG2_E)))
    la_hi, la_lo = _split(jnp.minimum(gz, 0.0) * (1.0 / GLA_TAU) - soft * (LN_2 / GLA_TAU))
    qa = _dot(hb, w[QA][...])
    ka = _dot(hb, w[KA][...])
    va = _dot(hb, w[VA][...]).astype(BF16)
    bc = _dot(tri_ref[...], la_hi) + _dot(tri_ref[...], la_lo)
    ra = _dot(hb, w[RA][...])
    qb = _dot(hb, w[QB][...]).astype(BF16) * jnp.asarray(SWA_DH ** -0.5, BF16)
    kb = _dot(hb, w[KB][...]).astype(BF16)
    vb = _dot(hb, w[VB][...]).astype(BF16)

    nc = ts // GLA_CHUNK
    chunk = lambda c: slice(c * GLA_CHUNK, (c + 1) * GLA_CHUNK)
    btot = [bc[(c + 1) * GLA_CHUNK - 1:(c + 1) * GLA_CHUNK, :] for c in range(nc)]
    pre = [jnp.zeros_like(btot[0])]
    for c in range(nc):
        pre.append(pre[c] + btot[c])
    rows = lambda vecs: jnp.concatenate([jnp.broadcast_to(v, (GLA_CHUNK, GLA_KEY_W)) for v in vecs], axis=0)
    qf = qa * (GLA_DK ** -0.5) * jnp.exp(bc)
    q_dec = qf.astype(BF16)
    q_in = (qf * rows([jnp.exp(pre[c]) for c in range(nc)])).astype(BF16)
    k_inv = (ka * jnp.exp(-bc)).astype(BF16)
    k_end = ka * jnp.exp(rows(btot) - bc)
    k_tile = (k_end * rows([jnp.exp(pre[nc] - pre[c + 1]) for c in range(nc)])).astype(BF16)
    k_cross = {(c, c2): (k_end[chunk(c2)] * jnp.exp(pre[c] - pre[c2 + 1])).astype(BF16)
               for c in range(nc) for c2 in range(c)}
    dec_tile = jnp.exp(pre[nc])
    causal = (lax.broadcasted_iota(jnp.int32, (ts, ts), 1) <= lax.broadcasted_iota(jnp.int32, (ts, ts), 0))

    def gla_scores(h):
        ks = slice(h * GLA_DK, (h + 1) * GLA_DK)
        att = []
        for c in range(nc):
            keys = [k_cross[(c, c2)][:, ks] for c2 in range(c)] + [k_inv[chunk(c), ks]]
            keys += [jnp.zeros((GLA_CHUNK, GLA_DK), BF16)] * (nc - 1 - c)
            att.append(_dot_nt(q_dec[chunk(c), ks], jnp.concatenate(keys, axis=0)))
        return jnp.where(causal, jnp.concatenate(att, axis=0), 0.0).astype(BF16)

    atts = [gla_scores(h) for h in range(GLA_HEADS)]
    gn = gn_ref[...]
    for h in range(GLA_HEADS):
        ks = slice(h * GLA_DK, (h + 1) * GLA_DK)
        vs = slice(h * GLA_DV, (h + 1) * GLA_DV)
        st = st_ref[h]
        o = _dot(atts[h], va[:, vs]) + _dot(q_in[:, ks], st.astype(BF16))
        dec_col = jnp.transpose(jnp.broadcast_to(dec_tile[:, ks], (8, GLA_DK)))[:, 0:1]
        st_ref[h] = st * dec_col + _dot_tn(k_tile[:, ks], va[:, vs])
        o = o * lax.rsqrt(jnp.mean(o * o, axis=-1, keepdims=True) + EPS) * gn
        oa_ref[:, vs] = o * _silu(ra[:, vs])

    wd = WINDOW
    k_all = jnp.concatenate([kprev_ref[...], kb], axis=0)
    v_all = jnp.concatenate([vprev_ref[...], vb], axis=0)
    kprev_ref[...] = kb[ts - wd:, :]
    vprev_ref[...] = vb[ts - wd:, :]
    lo_half = lax.broadcasted_iota(jnp.int32, (wd + ts, LANES), 1) < SWA_DH
    k_sw = pltpu.roll(k_all, SWA_DH, 1)
    v_sw = pltpu.roll(v_all, SWA_DH, 1)
    k2 = (jnp.where(lo_half, k_all, k_sw), jnp.where(lo_half, k_sw, k_all))
    v2 = (jnp.where(lo_half, v_all, v_sw), jnp.where(lo_half, v_sw, v_all))
    qi = lax.broadcasted_iota(jnp.int32, (wd, 2 * wd), 0) + wd
    kj = lax.broadcasted_iota(jnp.int32, (wd, 2 * wd), 1)
    band = (qi - kj >= 0) & (qi - kj < wd)
    band_first = band & (jnp.logical_not(first) | (kj >= wd)) if first is not False else band
    q_lo = lax.broadcasted_iota(jnp.int32, (wd, LANES), 1) < SWA_DH
    row0 = lax.broadcasted_iota(jnp.int32, (2 * wd, LANES), 0) == 0
    ones = jnp.ones((2 * wd, LANES), BF16)
    half = SWA_GROUP // 2
    combos = [(blk, hk) for blk in range(ts // wd) for hk in range(SWA_KV_HEADS)]

    def swa_probs(blk, hk):
        parts = []
        for p in range(half):
            qp = qb[blk * wd:(blk + 1) * wd, (hk * half + p) * LANES:(hk * half + p + 1) * LANES]
            parts += [jnp.where(q_lo, qp, jnp.zeros_like(qp)), jnp.where(q_lo, jnp.zeros_like(qp), qp)]
        s = _dot_nt(jnp.concatenate(parts, axis=0), k2[hk][blk * wd:blk * wd + 2 * wd])
        valid = band_first if blk == 0 else band
        es = []
        for r in range(SWA_GROUP):
            head = hk * SWA_GROUP + r
            sh = jnp.where(valid, s[r * wd:(r + 1) * wd], fill_ref[head:head + 1, :])
            es.append(jnp.exp(sh - jnp.max(sh, axis=-1, keepdims=True)).astype(BF16))
        return jnp.concatenate(es, axis=0)

    probs = [swa_probs(blk, hk) for blk, hk in combos]
    ga = _dot(hb, w[GA][...])
    gb = _dot(hb, w[GB][...])
    for (blk, hk), e in zip(combos, probs):
        vv = jnp.where(row0, jnp.zeros_like(ones), v2[hk][blk * wd:blk * wd + 2 * wd])
        o = _dot(e, jnp.concatenate([vv, ones], axis=1))
        for p in range(half):
            oe = o[(2 * p) * wd:(2 * p + 1) * wd]
            oo = o[(2 * p + 1) * wd:(2 * p + 2) * wd]
            num = jnp.where(q_lo, oe[:, :LANES], oo[:, :LANES])
            den = jnp.where(q_lo, oe[:, LANES:], oo[:, LANES:])
            ob_ref[blk * wd:(blk + 1) * wd, (hk * half + p) * LANES:(hk * half + p + 1) * LANES] = num / den

    merged = _sigmoid(ga) * oa_ref[...] + _sigmoid(gb) * ob_ref[...]
    o_ref[r0:r0 + ts, :] = x + gt_ref[...] * _dot(merged.astype(BF16), w[WO][...])


def _mixer_call(x2, g1, sc1, sh1, gt1, w2p, bgk, gn, fill, weights, bsz, seq):
    ns = seq // TM_MIX
    tok = pl.BlockSpec((TM_MIX, D_MODEL), lambda b, s: (b * ns + s, 0))
    vec = pl.BlockSpec((None, 1, D_MODEL), lambda b, s: (b, 0, 0))
    full = lambda a: pl.BlockSpec(a.shape, lambda b, s: (0, 0))
    hbm = pl.BlockSpec(memory_space=pl.ANY)
    idx = jnp.arange(MIX_SUB)
    tri = ((idx[:, None] // GLA_CHUNK == idx[None, :] // GLA_CHUNK) & (idx[None, :] <= idx[:, None])).astype(BF16)
    return pl.pallas_call(
        _mixer_kernel,
        grid=(bsz, ns),
        in_specs=[tok, full(g1), vec, vec, vec, full(w2p), full(bgk), full(tri), full(gn), full(fill)]
        + [hbm] * N_WEIGHTS,
        out_specs=tok,
        out_shape=jax.ShapeDtypeStruct(x2.shape, F32),
        scratch_shapes=[pltpu.VMEM(wt.shape, BF16) for wt in weights]
        + [pltpu.VMEM((GLA_HEADS, GLA_DK, GLA_DV), F32),
           pltpu.VMEM((WINDOW, SWA_KV_W), BF16), pltpu.VMEM((WINDOW, SWA_KV_W), BF16),
           pltpu.VMEM((MIX_SUB, GLA_VAL_W), F32), pltpu.VMEM((MIX_SUB, SWA_W), F32)],
        compiler_params=_params(("arbitrary", "arbitrary")),
        name="mixer",
    )(x2, g1, sc1, sh1, gt1, w2p, bgk, tri, gn, fill, *weights)


def _first_of(vals, target):
    idx = jnp.full(target.shape, len(vals) - 1, jnp.int32)
    for j in range(len(vals) - 2, -1, -1):
        idx = jnp.where(vals[j] == target, j, idx)
    return idx


def _rows_to_cols(a):
    return jnp.concatenate([jnp.transpose(a[:, i * LANES:(i + 1) * LANES]) for i in range(a.shape[1] // LANES)],
                           axis=0)


def _moe_kernel(x_ref, g2_ref, sc_ref, sh_ref, gt_ref, wrt_ref, brt_ref, upper_ref, wg_ref, wu_ref, wd_ref,
                gf_ref, o_ref, hs_ref, cws_ref, ys_ref):
    x1 = x_ref[...]
    h = _rms(x1, g2_ref[...]) * (1.0 + sc_ref[...]) + sh_ref[...]
    hb = h.astype(BF16)
    tm = x1.shape[0]
    neg = -jnp.inf

    lt = _dot_nt(wrt_ref[...], hb) + brt_ref[...]
    lrow = lambda k: lt[ROUTE_STRIDE * k:ROUTE_STRIDE * k + 1, :]
    lg = [lrow(g) for g in range(N_GROUPS)]
    gmax = functools.reduce(jnp.maximum, lg)
    g_w = 1.0 / functools.reduce(jnp.add, [jnp.exp(v - gmax) for v in lg])
    g_idx = _first_of(lg, gmax)
    le = []
    for j in range(EXPERTS_PER_GROUP):
        v = lrow(N_GROUPS + (N_GROUPS - 1) * EXPERTS_PER_GROUP + j)
        for g in range(N_GROUPS - 2, -1, -1):
            v = jnp.where(g_idx == g, lrow(N_GROUPS + g * EXPERTS_PER_GROUP + j), v)
        le.append(v)
    v1 = functools.reduce(jnp.maximum, le)
    i1 = _first_of(le, v1)
    le2 = [jnp.where(i1 == j, neg, le[j]) for j in range(EXPERTS_PER_GROUP)]
    v2 = functools.reduce(jnp.maximum, le2)
    i2 = _first_of(le2, v2)
    e2 = jnp.exp(v2 - v1)
    w1 = g_w / (1.0 + e2)
    w2 = g_w * e2 / (1.0 + e2)

    sub = lax.broadcasted_iota(jnp.int32, (ROW_PACK, tm), 0)
    onehot = (sub == g_idx).astype(BF16)
    rank = _dot(onehot, upper_ref[...])
    pcs, offs, run = [], [], 0
    pos = jnp.zeros((1, tm), F32)
    for g in range(N_GROUPS):
        mine = g_idx == g
        cnt = jnp.sum(mine.astype(F32), axis=1, keepdims=True)[0, 0].astype(jnp.int32)
        pcs.append(((cnt + (ROW_PACK - 1)) // ROW_PACK) * ROW_PACK)
        offs.append(run)
        pos = pos + jnp.where(mine, rank[g:g + 1, :] + jnp.asarray(run, F32), 0.0)
        run = run + pcs[g]
    rows = tm + MOE_PAD
    perm = (lax.broadcasted_iota(jnp.int32, (rows, tm), 0) == pos.astype(jnp.int32)).astype(BF16)
    hs_ref[0:rows, :] = _dot(perm, hb).astype(BF16)
    c_hi, c_lo = _split(jnp.where(sub == i1, w1, 0.0) + jnp.where(sub == i2, w2, 0.0))
    cws_ref[0:rows, :] = _rows_to_cols(_dot_nt(c_hi, perm) + _dot_nt(c_lo, perm))
    pos_col = _rows_to_cols(jnp.broadcast_to(pos, (8, tm)))[:, 0:1].astype(jnp.int32)
    sel_t = (lax.broadcasted_iota(jnp.int32, (tm, rows), 1) == pos_col).astype(BF16)
    hs_ref[rows:, :] = jnp.zeros((MOE_BM, D_MODEL), BF16)
    cws_ref[rows:, :] = jnp.zeros((MOE_BM, ROW_PACK), F32)
    ys_ref[...] = jnp.zeros_like(ys_ref)

    for g in range(N_GROUPS):
        def block(k, carry, g=g):
            start = pl.multiple_of(offs[g] + k * MOE_BM, ROW_PACK)
            hblk = hs_ref[pl.ds(start, MOE_BM), :]
            inside = (start + lax.broadcasted_iota(jnp.int32, (MOE_BM, ROW_PACK), 0)) < offs[g] + pcs[g]
            cw = jnp.where(inside, cws_ref[pl.ds(start, MOE_BM), :], 0.0)
            experts = [g * EXPERTS_PER_GROUP + j for j in range(EXPERTS_PER_GROUP)]
            gates = [_dot(hblk, wg_ref[e]) for e in experts]
            ups = [_dot(hblk, wu_ref[e]) for e in experts]
            hids = [(_silu(gates[j]) * ups[j] * cw[:, j:j + 1]).astype(BF16) for j in range(EXPERTS_PER_GROUP)]
            y = _dot(hids[0], wd_ref[experts[0]])
            for j in range(1, EXPERTS_PER_GROUP):
                y = y + _dot(hids[j], wd_ref[experts[j]])
            ys_ref[pl.ds(start, MOE_BM), :] += y
            return carry
        lax.fori_loop(0, (pcs[g] + (MOE_BM - 1)) // MOE_BM, block, 0)

    y = _dot(sel_t, ys_ref[0:rows, :].astype(BF16))
    o_ref[...] = _rms(x1 + gt_ref[...] * y, gf_ref[...])


def _load_bf16(src, dst, stage, sem):
    n = src.shape[0] // WEIGHT_CHUNK
    copy = lambda i: pltpu.make_async_copy(src.at[pl.ds(i * WEIGHT_CHUNK, WEIGHT_CHUNK)], stage.at[i % 2],
                                           sem.at[i % 2])
    copy(0).start()
    for i in range(n):
        if i + 1 < n:
            copy(i + 1).start()
        copy(i).wait()
        dst[pl.ds(i * WEIGHT_CHUNK, WEIGHT_CHUNK)] = stage[i % 2].astype(BF16)


def _moe_body(x_ref, g2_ref, sc_ref, sh_ref, gt_ref, wrt_ref, brt_ref, upper_ref, wg_hbm, wu_hbm, wd_hbm, gf_ref,
              o_ref, wg_ref, wu_ref, wd_ref, hs_ref, cws_ref, ys_ref, stage_in, stage_out, sem):
    @pl.when(pl.program_id(0) == 0)
    def _():
        _load_bf16(wg_hbm, wg_ref, stage_in, sem)
        _load_bf16(wu_hbm, wu_ref, stage_in, sem)
        _load_bf16(wd_hbm, wd_ref, stage_out, sem)

    _moe_kernel(x_ref, g2_ref, sc_ref, sh_ref, gt_ref, wrt_ref, brt_ref, upper_ref, wg_ref, wu_ref, wd_ref,
                gf_ref, o_ref, hs_ref, cws_ref, ys_ref)


def _moe_call(x1, g2, sc2, sh2, gt2, wrt, brt, wg, wu, wd, gf, seq):
    t = x1.shape[0]
    per_b = seq // TM_MOE
    tok = pl.BlockSpec((TM_MOE, D_MODEL), lambda i: (i, 0))
    vec = pl.BlockSpec((None, 1, D_MODEL), lambda i: (i // per_b, 0, 0))
    row = pl.BlockSpec((1, D_MODEL), lambda i: (0, 0))
    full = lambda a: pl.BlockSpec(a.shape, lambda i: (0,) * a.ndim)
    hbm = pl.BlockSpec(memory_space=pl.ANY)
    srows = TM_MOE + MOE_PAD + MOE_BM
    upper = jnp.triu(jnp.ones((TM_MOE, TM_MOE), BF16), k=1)
    return pl.pallas_call(
        _moe_body,
        grid=(t // TM_MOE,),
        in_specs=[tok, row, vec, vec, vec, full(wrt), full(brt), full(upper), hbm, hbm, hbm, row],
        out_specs=tok,
        out_shape=jax.ShapeDtypeStruct((t, D_MODEL), F32),
        scratch_shapes=[pltpu.VMEM(wg.shape, BF16), pltpu.VMEM(wu.shape, BF16), pltpu.VMEM(wd.shape, BF16),
                        pltpu.VMEM((srows, D_MODEL), BF16), pltpu.VMEM((srows, ROW_PACK), F32),
                        pltpu.VMEM((srows, D_MODEL), F32),
                        pltpu.VMEM((2, WEIGHT_CHUNK) + wg.shape[1:], F32),
                        pltpu.VMEM((2, WEIGHT_CHUNK) + wd.shape[1:], F32),
                        pltpu.SemaphoreType.DMA((2,))],
        compiler_params=_params(("arbitrary",)),
        name="moe",
    )(x1, g2, sc2, sh2, gt2, wrt, brt, upper, wg, wu, wd, gf)


def kernel(x, c, w_ada, b_ada, norm1_g, w_in, w_gk2, b_gk, gla_norm_g, sink, w_o, norm2_g,
           w_group, b_group, w_router, b_router, w_gate, w_up, w_down, norm_f_g):
    bsz, seq, d = x.shape
    depth = w_ada.shape[0]
    assert depth == 1, "the final norm is fused into the (single) layer's MoE kernel"
    x2 = x.reshape(bsz * seq, d)
    for l in range(depth):
        mod = _mod_call(c, w_ada[l], b_ada[l])
        sh1, sc1, gt1, sh2, sc2, gt2 = [m.reshape(bsz, 1, d) for m in jnp.split(mod, N_MOD, axis=-1)]

        w_parts = _split_cols(w_in[l])
        w_parts[GATE] = jnp.pad(w_parts[GATE], ((0, 0), (0, LANES - GLA_GATE_RANK)))
        weights = [wt.astype(BF16) for wt in w_parts + [w_o[l]]]
        w2p = jnp.pad(w_gk2[l], ((0, LANES - GLA_GATE_RANK), (0, 0))).astype(BF16)
        fill = jnp.full((SWA_HEADS, 2 * WINDOW), -jnp.inf, F32).at[:, 0].set(sink[l])
        x2 = _mixer_call(x2, norm1_g[l].reshape(1, d), sc1, sh1, gt1, w2p, b_gk[l].reshape(1, GLA_KEY_W),
                         gla_norm_g[l].reshape(1, GLA_DV), fill, weights, bsz, seq)

        n_logit = N_GROUPS + N_EXPERTS
        wrt = jnp.zeros((n_logit, ROUTE_STRIDE, d), F32).at[:, 0, :].set(
            jnp.concatenate([w_group[l], w_router[l]], axis=1).T).reshape(n_logit * ROUTE_STRIDE, d).astype(BF16)
        brt = jnp.zeros((n_logit, ROUTE_STRIDE), F32).at[:, 0].set(
            jnp.concatenate([b_group[l], b_router[l]])).reshape(n_logit * ROUTE_STRIDE, 1)
        x2 = _moe_call(x2, norm2_g[l].reshape(1, d), sc2, sh2, gt2, wrt, brt,
                       w_gate[l], w_up[l], w_down[l], norm_f_g.reshape(1, d), seq)
    return x2.reshape(bsz, seq, d)


def _split_cols(w):
    parts, start = [], 0
    for width in IN_WIDTHS:
        parts.append(w[:, start:start + width])
        start += width
    return parts
```

```python
import functools

import jax
import jax.numpy as jnp
from jax import lax
from jax.experimental import pallas as pl
from jax.experimental.pallas import tpu as pltpu

F32 = jnp.float32
BF16 = jnp.bfloat16

D_MODEL = 1024
GLA_HEADS = 4
GLA_DK = 128
GLA_DV = 256
GLA_KEY_W = GLA_HEADS * GLA_DK
GLA_VAL_W = GLA_HEADS * GLA_DV
GLA_GATE_RANK = 16
GLA_TAU = 16.0
GLA_CHUNK = 64
SWA_HEADS = 16
SWA_KV_HEADS = 2
SWA_GROUP = SWA_HEADS // SWA_KV_HEADS
SWA_DH = 64
SWA_W = SWA_HEADS * SWA_DH
SWA_KV_W = SWA_KV_HEADS * SWA_DH
WINDOW = 128
N_GROUPS = 4
EXPERTS_PER_GROUP = 4
N_EXPERTS = N_GROUPS * EXPERTS_PER_GROUP
D_EXPERT = 256
N_MOD = 6
EPS = 1e-6
IN_WIDTHS = (GLA_KEY_W, GLA_KEY_W, GLA_VAL_W, GLA_VAL_W, GLA_GATE_RANK,
             SWA_W, SWA_KV_W, SWA_KV_W, D_MODEL, D_MODEL)

LANES = 128
LOG2_E = 1.4426950408889634
LN_2 = 0.6931471805599453
VMEM_LIMIT = 56 * 1024 * 1024

TM_MIX = 1024
MIX_SUB = 256
SUBS = TM_MIX // MIX_SUB
ROW_PACK = 16
SORT_ROWS = MIX_SUB + 64
SEG_BITS = (256, 128, 64, 32, 16)
HS_W = D_MODEL + LANES
TM_EXP = 512
META = 2 * N_GROUPS
ROUTE_STRIDE = 8
TN_MOD = 512


def _params(sem):
    return pltpu.CompilerParams(dimension_semantics=sem, vmem_limit_bytes=VMEM_LIMIT)


def _split(a):
    hi = a.astype(BF16)
    lo = (a - hi.astype(F32)).astype(BF16)
    return hi, lo


def _dot(a, b):
    return jnp.dot(a, b, preferred_element_type=F32)


def _dot_nt(a, b):
    return lax.dot_general(a, b, (((1,), (1,)), ((), ())), preferred_element_type=F32)


def _dot_tn(a, b):
    return lax.dot_general(a, b, (((0,), (0,)), ((), ())), preferred_element_type=F32)


def _dot3(a, b):
    a_hi, a_lo = _split(a)
    b_hi, b_lo = _split(b)
    return _dot(a_hi, b_hi) + _dot(a_hi, b_lo) + _dot(a_lo, b_hi)


def _sigmoid(x):
    return 1.0 / (1.0 + jnp.exp(-x))


def _silu(x):
    return x * _sigmoid(x)


def _rms(x, g):
    return x * lax.rsqrt(jnp.mean(x * x, axis=-1, keepdims=True) + EPS) * g


def _first_of(vals, target):
    idx = jnp.full(target.shape, len(vals) - 1, jnp.int32)
    for j in range(len(vals) - 2, -1, -1):
        idx = jnp.where(vals[j] == target, j, idx)
    return idx


def _rows_to_cols(a):
    return jnp.concatenate([jnp.transpose(a[:, i * LANES:(i + 1) * LANES]) for i in range(a.shape[1] // LANES)],
                           axis=0)


def _segment_copies(pcs, src_rows, dst_rows, make_copy, action):
    for g in range(N_GROUPS):
        for bit in SEG_BITS:
            done = pcs[g] & ~(2 * bit - 1)

            @pl.when((pcs[g] & bit) != 0)
            def _(g=g, bit=bit, done=done):
                cp = make_copy(g, pl.multiple_of(src_rows[g] + done, ROW_PACK),
                               pl.multiple_of(dst_rows[g] + done, ROW_PACK), bit)
                cp.start() if action == "start" else cp.wait()


def _mod_kernel(c_ref, w_ref, b_ref, o_ref):
    o_ref[...] = _dot3(_silu(c_ref[...]), w_ref[...]) + b_ref[...]


def _mod_call(c, w_ada, b_ada):
    bsz = c.shape[0]
    n = w_ada.shape[1]
    return pl.pallas_call(
        _mod_kernel,
        grid=(n // TN_MOD,),
        in_specs=[pl.BlockSpec((bsz, D_MODEL), lambda j: (0, 0)),
                  pl.BlockSpec((D_MODEL, TN_MOD), lambda j: (0, j)),
                  pl.BlockSpec((1, TN_MOD), lambda j: (0, j))],
        out_specs=pl.BlockSpec((bsz, TN_MOD), lambda j: (0, j)),
        out_shape=jax.ShapeDtypeStruct((bsz, n), F32),
        compiler_params=_params(("parallel",)),
        name="mod",
    )(c, w_ada, b_ada.reshape(1, n))


QA, KA, VA, RA, GATE, QB, KB, VB, GA, GB, WO = range(11)
N_WEIGHTS = 11
N_VMEM_IN = 18


def _mixer_kernel(x_ref, g1_ref, sc1_ref, sh1_ref, gt1_ref, w2_ref, bgk_ref, tri_ref, gn_ref, fill_ref,
                  g2_ref, sc2_ref, sh2_ref, wrt_ref, brt_ref, upper_ref, wgf_ref, wuf_ref, wdf_ref, *rest):
    w_hbm = rest[:N_WEIGHTS]
    x1_ref, hs_hbm, pos_ref, meta_ref, wgb_ref, wub_ref, wdb_ref = rest[N_WEIGHTS:N_WEIGHTS + 7]
    w = rest[N_WEIGHTS + 7:2 * N_WEIGHTS + 7]
    st_ref, kprev_ref, vprev_ref, oa_ref, ob_ref, stage_ref, zero_ref, base_ref, sem = rest[2 * N_WEIGHTS + 7:]
    first = pl.program_id(1) == 0
    step = pl.program_id(0) * pl.num_programs(1) + pl.program_id(1)
    n_steps = pl.num_programs(0) * pl.num_programs(1)

    @pl.when(step == 0)
    def _():
        for src, dst in zip(w_hbm, w):
            pltpu.sync_copy(src, dst)
        for g in range(N_GROUPS):
            base_ref[g] = 0

    @pl.when(first)
    def _():
        st_ref[...] = jnp.zeros_like(st_ref)
        kprev_ref[...] = jnp.zeros_like(kprev_ref)
        vprev_ref[...] = jnp.zeros_like(vprev_ref)

    wgb_ref[...] = wgf_ref[...].astype(BF16)
    wub_ref[...] = wuf_ref[...].astype(BF16)
    wdb_ref[...] = wdf_ref[...].astype(BF16)

    def stage_copy(slot):
        return lambda g, src_row, dst_row, size: pltpu.make_async_copy(
            stage_ref.at[slot, pl.ds(src_row, size), :], hs_hbm.at[g, pl.ds(dst_row, size), :], sem.at[slot])

    zeros4 = [0] * N_GROUPS

    def publish(sub, routed):
        sorted_rows, pos, pcs, offs = routed
        pos_ref[sub] = jnp.broadcast_to(pos, (8, MIX_SUB))
        m0 = (step * SUBS + sub) * META

        @pl.when(step > 0)
        def _():
            prev = [meta_ref[m0 - SUBS * META + N_GROUPS + g] for g in range(N_GROUPS)]
            _segment_copies(prev, zeros4, zeros4, stage_copy(sub), "wait")

        stage_ref[sub] = sorted_rows
        bases = [base_ref[g] for g in range(N_GROUPS)]
        for g in range(N_GROUPS):
            meta_ref[m0 + g] = bases[g]
            meta_ref[m0 + N_GROUPS + g] = pcs[g]
            base_ref[g] = bases[g] + pcs[g]
        _segment_copies(pcs, offs, bases, stage_copy(sub), "start")

        @pl.when(step == n_steps - 1)
        def _():
            _segment_copies(pcs, zeros4, zeros4, stage_copy(sub), "wait")

    routing = None
    for sub in range(SUBS):
        r0 = sub * MIX_SUB
        x1, routed = _mixer_tile(x_ref[r0:r0 + MIX_SUB, :], first if sub == 0 else False, g1_ref, sc1_ref,
                                 sh1_ref, gt1_ref, w2_ref, bgk_ref, tri_ref, gn_ref, fill_ref, w, st_ref,
                                 kprev_ref, vprev_ref, oa_ref, ob_ref, side=routing)
        x1_ref[r0:r0 + MIX_SUB, :] = x1
        if routing is not None:
            publish(sub - 1, routed)
        routing = _route_and_sort(x1, g2_ref, sc2_ref, sh2_ref, wrt_ref, brt_ref, upper_ref)
    next(routing)
    next(routing)
    publish(SUBS - 1, next(routing))

    @pl.when(step == n_steps - 1)
    def _():
        zero_ref[...] = jnp.zeros_like(zero_ref)
        tails = [pltpu.make_async_copy(zero_ref, hs_hbm.at[g, pl.ds(pl.multiple_of(base_ref[g], ROW_PACK), TM_EXP), :],
                                       sem.at[0]) for g in range(N_GROUPS)]
        for cp in tails:
            cp.start()
        for cp in tails:
            cp.wait()


def _mixer_tile(x, first, g1_ref, sc_ref, sh_ref, gt_ref, w2_ref, bgk_ref, tri_ref, gn_ref, fill_ref,
                w, st_ref, kprev_ref, vprev_ref, oa_ref, ob_ref, side=None):
    ts = MIX_SUB
    advance = (lambda: next(side)) if side is not None else (lambda: None)
    hb = (_rms(x, g1_ref[...]) * (1.0 + sc_ref[...]) + sh_ref[...]).astype(BF16)

    z = _dot(hb, w[GATE][...])
    gz = _dot(z.astype(BF16), w2_ref[...]) + bgk_ref[...]
    soft = jnp.log2(1.0 + jnp.exp2(jnp.abs(gz) * (-LOG2_E)))
    la_hi, la_lo = _split(jnp.minimum(gz, 0.0) * (1.0 / GLA_TAU) - soft * (LN_2 / GLA_TAU))
    qa = _dot(hb, w[QA][...])
    ka = _dot(hb, w[KA][...])
    advance()
    va = _dot(hb, w[VA][...]).astype(BF16)
    bc = _dot(tri_ref[...], la_hi) + _dot(tri_ref[...], la_lo)
    ra = _dot(hb, w[RA][...])
    advance()
    qb = _dot(hb, w[QB][...]).astype(BF16) * jnp.asarray(SWA_DH ** -0.5, BF16)
    kb = _dot(hb, w[KB][...]).astype(BF16)
    vb = _dot(hb, w[VB][...]).astype(BF16)

    nc = ts // GLA_CHUNK
    chunk = lambda c: slice(c * GLA_CHUNK, (c + 1) * GLA_CHUNK)
    btot = [bc[(c + 1) * GLA_CHUNK - 1:(c + 1) * GLA_CHUNK, :] for c in range(nc)]
    pre = [jnp.zeros_like(btot[0])]
    for c in range(nc):
        pre.append(pre[c] + btot[c])
    rows = lambda vecs: jnp.concatenate([jnp.broadcast_to(v, (GLA_CHUNK, GLA_KEY_W)) for v in vecs], axis=0)
    qf = qa * (GLA_DK ** -0.5) * jnp.exp(bc)
    q_dec = qf.astype(BF16)
    q_in = (qf * rows([jnp.exp(pre[c]) for c in range(nc)])).astype(BF16)
    k_inv = (ka * jnp.exp(-bc)).astype(BF16)
    k_end = ka * jnp.exp(rows(btot) - bc)
    k_tile = (k_end * rows([jnp.exp(pre[nc] - pre[c + 1]) for c in range(nc)])).astype(BF16)
    k_cross = {(c, c2): (k_end[chunk(c2)] * jnp.exp(pre[c] - pre[c2 + 1])).astype(BF16)
               for c in range(nc) for c2 in range(c)}
    dec_tile = jnp.exp(pre[nc])
    causal = (lax.broadcasted_iota(jnp.int32, (ts, ts), 1) <= lax.broadcasted_iota(jnp.int32, (ts, ts), 0))

    def gla_scores(h):
        ks = slice(h * GLA_DK, (h + 1) * GLA_DK)
        att = []
        for c in range(nc):
            keys = [k_cross[(c, c2)][:, ks] for c2 in range(c)] + [k_inv[chunk(c), ks]]
            keys += [jnp.zeros((GLA_CHUNK, GLA_DK), BF16)] * (nc - 1 - c)
            att.append(_dot_nt(q_dec[chunk(c), ks], jnp.concatenate(keys, axis=0)))
        return jnp.where(causal, jnp.concatenate(att, axis=0), 0.0).astype(BF16)

    atts = [gla_scores(h) for h in range(GLA_HEADS)]
    gn = gn_ref[...]
    for h in range(GLA_HEADS):
        ks = slice(h * GLA_DK, (h + 1) * GLA_DK)
        vs = slice(h * GLA_DV, (h + 1) * GLA_DV)
        st = st_ref[h]
        o = _dot(atts[h], va[:, vs]) + _dot(q_in[:, ks], st.astype(BF16))
        dec_col = jnp.transpose(jnp.broadcast_to(dec_tile[:, ks], (8, GLA_DK)))[:, 0:1]
        st_ref[h] = st * dec_col + _dot_tn(k_tile[:, ks], va[:, vs])
        o = o * lax.rsqrt(jnp.mean(o * o, axis=-1, keepdims=True) + EPS) * gn
        oa_ref[:, vs] = o * _silu(ra[:, vs])

    wd = WINDOW
    k_all = jnp.concatenate([kprev_ref[...], kb], axis=0)
    v_all = jnp.concatenate([vprev_ref[...], vb], axis=0)
    kprev_ref[...] = kb[ts - wd:, :]
    vprev_ref[...] = vb[ts - wd:, :]
    lo_half = lax.broadcasted_iota(jnp.int32, (wd + ts, LANES), 1) < SWA_DH
    k_sw = pltpu.roll(k_all, SWA_DH, 1)
    v_sw = pltpu.roll(v_all, SWA_DH, 1)
    k2 = (jnp.where(lo_half, k_all, k_sw), jnp.where(lo_half, k_sw, k_all))
    v2 = (jnp.where(lo_half, v_all, v_sw), jnp.where(lo_half, v_sw, v_all))
    qi = lax.broadcasted_iota(jnp.int32, (wd, 2 * wd), 0) + wd
    kj = lax.broadcasted_iota(jnp.int32, (wd, 2 * wd), 1)
    band = (qi - kj >= 0) & (qi - kj < wd)
    band_first = band & (jnp.logical_not(first) | (kj >= wd)) if first is not False else band
    q_lo = lax.broadcasted_iota(jnp.int32, (wd, LANES), 1) < SWA_DH
    row0 = lax.broadcasted_iota(jnp.int32, (2 * wd, LANES), 0) == 0
    ones = jnp.ones((2 * wd, LANES), BF16)
    half = SWA_GROUP // 2
    combos = [(blk, hk) for blk in range(ts // wd) for hk in range(SWA_KV_HEADS)]

    def swa_probs(blk, hk):
        parts = []
        for p in range(half):
            qp = qb[blk * wd:(blk + 1) * wd, (hk * half + p) * LANES:(hk * half + p + 1) * LANES]
            parts += [jnp.where(q_lo, qp, jnp.zeros_like(qp)), jnp.where(q_lo, jnp.zeros_like(qp), qp)]
        s = _dot_nt(jnp.concatenate(parts, axis=0), k2[hk][blk * wd:blk * wd + 2 * wd])
        valid = band_first if blk == 0 else band
        es = []
        for r in range(SWA_GROUP):
            head = hk * SWA_GROUP + r
            sh = jnp.where(valid, s[r * wd:(r + 1) * wd], fill_ref[head:head + 1, :])
            es.append(jnp.exp(sh - jnp.max(sh, axis=-1, keepdims=True)).astype(BF16))
        return jnp.concatenate(es, axis=0)

    probs = [swa_probs(blk, hk) for blk, hk in combos]
    side_out = advance()
    ga = _dot(hb, w[GA][...])
    gb = _dot(hb, w[GB][...])
    for (blk, hk), e in zip(combos, probs):
        vv = jnp.where(row0, jnp.zeros_like(ones), v2[hk][blk * wd:blk * wd + 2 * wd])
        o = _dot(e, jnp.concatenate([vv, ones], axis=1))
        for p in range(half):
            oe = o[(2 * p) * wd:(2 * p + 1) * wd]
            oo = o[(2 * p + 1) * wd:(2 * p + 2) * wd]
            num = jnp.where(q_lo, oe[:, :LANES], oo[:, :LANES])
            den = jnp.where(q_lo, oe[:, LANES:], oo[:, LANES:])
            ob_ref[blk * wd:(blk + 1) * wd, (hk * half + p) * LANES:(hk * half + p + 1) * LANES] = num / den

    merged = _sigmoid(ga) * oa_ref[...] + _sigmoid(gb) * ob_ref[...]
    return x + gt_ref[...] * _dot(merged.astype(BF16), w[WO][...]), side_out


def _route_and_sort(x1, g2_ref, sc_ref, sh_ref, wrt_ref, brt_ref, upper_ref):
    tm = x1.shape[0]
    hb = (_rms(x1, g2_ref[...]) * (1.0 + sc_ref[...]) + sh_ref[...]).astype(BF16)
    neg = -jnp.inf
    lt = _dot_nt(wrt_ref[...], hb) + brt_ref[...]
    yield None
    lrow = lambda k: lt[ROUTE_STRIDE * k:ROUTE_STRIDE * k + 1, :]
    lg = [lrow(g) for g in range(N_GROUPS)]
    gmax = functools.reduce(jnp.maximum, lg)
    g_w = 1.0 / functools.reduce(jnp.add, [jnp.exp(v - gmax) for v in lg])
    g_idx = _first_of(lg, gmax)
    le = []
    for j in range(EXPERTS_PER_GROUP):
        v = lrow(N_GROUPS + (N_GROUPS - 1) * EXPERTS_PER_GROUP + j)
        for g in range(N_GROUPS - 2, -1, -1):
            v = jnp.where(g_idx == g, lrow(N_GROUPS + g * EXPERTS_PER_GROUP + j), v)
        le.append(v)
    v1 = functools.reduce(jnp.maximum, le)
    i1 = _first_of(le, v1)
    le2 = [jnp.where(i1 == j, neg, le[j]) for j in range(EXPERTS_PER_GROUP)]
    v2 = functools.reduce(jnp.maximum, le2)
    i2 = _first_of(le2, v2)
    e2 = jnp.exp(v2 - v1)
    w1 = g_w / (1.0 + e2)
    w2 = g_w * e2 / (1.0 + e2)

    sub = lax.broadcasted_iota(jnp.int32, (ROW_PACK, tm), 0)
    rank = _dot((sub == g_idx).astype(BF16), upper_ref[...])
    yield None
    pcs, offs, run = [], [], 0
    pos = jnp.zeros((1, tm), F32)
    for g in range(N_GROUPS):
        mine = g_idx == g
        cnt = jnp.sum(mine.astype(F32), axis=1, keepdims=True)[0, 0].astype(jnp.int32)
        pcs.append(((cnt + (ROW_PACK - 1)) // ROW_PACK) * ROW_PACK)
        offs.append(run)
        pos = pos + jnp.where(mine, rank[g:g + 1, :] + jnp.asarray(run, F32), 0.0)
        run = run + pcs[g]
    perm = (lax.broadcasted_iota(jnp.int32, (SORT_ROWS, tm), 0) == pos.astype(jnp.int32)).astype(BF16)
    cw = jnp.where(sub == i1, w1, 0.0) + jnp.where(sub == i2, w2, 0.0)
    cw_hi = cw.astype(BF16).astype(F32)
    packed = cw_hi + pltpu.roll(cw - cw_hi, EXPERTS_PER_GROUP, 0)
    cw_cols = _rows_to_cols(jnp.concatenate([packed, jnp.zeros((LANES - ROW_PACK, tm), F32)], axis=0))
    sorted_rows = _dot(perm, jnp.concatenate([hb, cw_cols.astype(BF16)], axis=1)).astype(BF16)
    yield sorted_rows, pos, pcs, offs


def _mixer_call(x2, vecs1, vecs2, w2p, bgk, gn, fill, wrt, brt, weights, experts_f32, cap, bsz, seq):
    g1, sc1, sh1, gt1 = vecs1
    g2, sc2, sh2 = vecs2
    ns = seq // TM_MIX
    n_steps = bsz * ns
    tok = pl.BlockSpec((TM_MIX, D_MODEL), lambda b, s: (b * ns + s, 0))
    vec = pl.BlockSpec((None, 1, D_MODEL), lambda b, s: (b, 0, 0))
    full = lambda a: pl.BlockSpec(a.shape, lambda b, s: (0,) * a.ndim)
    hbm = pl.BlockSpec(memory_space=pl.ANY)
    slab = lambda a: pl.BlockSpec((a.shape[0] // n_steps, a.shape[1]), lambda b, s: (b * ns + s, 0))
    idx = jnp.arange(MIX_SUB)
    tri = ((idx[:, None] // GLA_CHUNK == idx[None, :] // GLA_CHUNK) & (idx[None, :] <= idx[:, None])).astype(BF16)
    upper = (idx[:, None] < idx[None, :]).astype(BF16)
    ex2d = [e.reshape(-1, e.shape[-1]) for e in experts_f32]
    n_sub = n_steps * SUBS
    vmem_in = [x2, g1, sc1, sh1, gt1, w2p, bgk, tri, gn, fill, g2, sc2, sh2, wrt, brt, upper]
    assert len(vmem_in) + len(ex2d) == N_VMEM_IN + 1
    outs = pl.pallas_call(
        _mixer_kernel,
        grid=(bsz, ns),
        in_specs=[tok, full(g1), vec, vec, vec, full(w2p), full(bgk), full(tri), full(gn), full(fill),
                  full(g2), vec, vec, full(wrt), full(brt), full(upper)]
        + [slab(e) for e in ex2d] + [hbm] * N_WEIGHTS,
        out_specs=[tok, hbm,
                   pl.BlockSpec((SUBS, 8, MIX_SUB), lambda b, s: (b * ns + s, 0, 0)),
                   pl.BlockSpec(memory_space=pltpu.SMEM)] + [slab(e) for e in ex2d],
        out_shape=[jax.ShapeDtypeStruct(x2.shape, F32),
                   jax.ShapeDtypeStruct((N_GROUPS, cap, HS_W), BF16),
                   jax.ShapeDtypeStruct((n_sub, 8, MIX_SUB), F32),
                   jax.ShapeDtypeStruct((n_sub * META,), jnp.int32)]
        + [jax.ShapeDtypeStruct(e.shape, BF16) for e in ex2d],
        scratch_shapes=[pltpu.VMEM(wt.shape, BF16) for wt in weights]
        + [pltpu.VMEM((GLA_HEADS, GLA_DK, GLA_DV), F32),
           pltpu.VMEM((WINDOW, SWA_KV_W), BF16), pltpu.VMEM((WINDOW, SWA_KV_W), BF16),
           pltpu.VMEM((MIX_SUB, GLA_VAL_W), F32), pltpu.VMEM((MIX_SUB, SWA_W), F32),
           pltpu.VMEM((SUBS, SORT_ROWS, HS_W), BF16), pltpu.VMEM((TM_EXP, HS_W), BF16),
           pltpu.SMEM((N_GROUPS,), jnp.int32), pltpu.SemaphoreType.DMA((SUBS,))],
        compiler_params=_params(("arbitrary", "arbitrary")),
        name="mixer",
    )(*vmem_in, *ex2d, *weights)
    x1, hs, pos, meta = outs[:4]
    experts_bf16 = [o.reshape(e.shape) for o, e in zip(outs[4:], experts_f32)]
    return x1, hs, pos, meta, experts_bf16


def _experts_kernel(grp_ref, blk_ref, valid_ref, hs_ref, wg_ref, wu_ref, wd_ref, ys_ref):
    @pl.when(valid_ref[pl.program_id(0)] != 0)
    def _():
        hx = hs_ref[...]
        h = hx[:, :D_MODEL]
        cw = hx[:, D_MODEL:].astype(F32)
        gates = [_dot(h, wg_ref[j]) for j in range(EXPERTS_PER_GROUP)]
        ups = [_dot(h, wu_ref[j]) for j in range(EXPERTS_PER_GROUP)]
        hids = [(_silu(gates[j]) * ups[j]
                 * (cw[:, j:j + 1] + cw[:, EXPERTS_PER_GROUP + j:EXPERTS_PER_GROUP + j + 1])).astype(BF16)
                for j in range(EXPERTS_PER_GROUP)]
        y = _dot(hids[0], wd_ref[0])
        for j in range(1, EXPERTS_PER_GROUP):
            y = y + _dot(hids[j], wd_ref[j])
        ys_ref[...] = y.astype(BF16)


def _experts_call(grp, blk, valid, hs, wg, wu, wd):
    n_work = grp.shape[0]
    rows = lambda width: pl.BlockSpec((None, TM_EXP, width), lambda i, grp, blk, valid: (grp[i], blk[i], 0))
    wspec = lambda a: pl.BlockSpec((EXPERTS_PER_GROUP,) + a.shape[1:], lambda i, grp, blk, valid: (grp[i], 0, 0))
    return pl.pallas_call(
        _experts_kernel,
        grid_spec=pltpu.PrefetchScalarGridSpec(
            num_scalar_prefetch=3,
            grid=(n_work,),
            in_specs=[rows(HS_W), wspec(wg), wspec(wu), wspec(wd)],
            out_specs=rows(D_MODEL)),
        out_shape=jax.ShapeDtypeStruct(hs.shape[:2] + (D_MODEL,), BF16),
        compiler_params=_params(("arbitrary",)),
        name="experts",
    )(grp, blk, valid, hs, wg, wu, wd)


def _final_kernel(meta_ref, x1_ref, pos_ref, gt_ref, gf_ref, ys_hbm, o_ref, ybuf_ref, sem):
    step = pl.program_id(0)

    def fetch(s, action):
        half = s % 2
        for sub in range(SUBS):
            m0 = (s * SUBS + sub) * META
            bases = [meta_ref[m0 + g] for g in range(N_GROUPS)]
            pcs = [meta_ref[m0 + N_GROUPS + g] for g in range(N_GROUPS)]
            offs, run = [], 0
            for g in range(N_GROUPS):
                offs.append(run)
                run = run + pcs[g]
            _segment_copies(pcs, bases, offs, lambda g, src_row, dst_row, size, sub=sub: pltpu.make_async_copy(
                ys_hbm.at[g, pl.ds(src_row, size), :], ybuf_ref.at[half, sub, pl.ds(dst_row, size), :],
                sem.at[half]), action)

    @pl.when(step == 0)
    def _():
        ybuf_ref[...] = jnp.zeros_like(ybuf_ref)
        fetch(step, "start")

    @pl.when(step + 1 < pl.num_programs(0))
    def _():
        fetch(step + 1, "start")

    fetch(step, "wait")
    half = step % 2
    for sub in range(SUBS):
        r0 = sub * MIX_SUB
        pos_col = _rows_to_cols(pos_ref[sub])[:, 0:1].astype(jnp.int32)
        sel = (lax.broadcasted_iota(jnp.int32, (MIX_SUB, SORT_ROWS), 1) == pos_col).astype(BF16)
        y = _dot(sel, ybuf_ref[half, sub])
        o_ref[r0:r0 + MIX_SUB, :] = _rms(x1_ref[r0:r0 + MIX_SUB, :] + gt_ref[...] * y, gf_ref[...])


def _final_call(meta, x1, pos, gt2, gf, ys, seq):
    t = x1.shape[0]
    per_b = seq // TM_MIX
    tok = pl.BlockSpec((TM_MIX, D_MODEL), lambda i, meta: (i, 0))
    return pl.pallas_call(
        _final_kernel,
        grid_spec=pltpu.PrefetchScalarGridSpec(
            num_scalar_prefetch=1,
            grid=(t // TM_MIX,),
            in_specs=[tok, pl.BlockSpec((SUBS, 8, MIX_SUB), lambda i, meta: (i, 0, 0)),
                      pl.BlockSpec((None, 1, D_MODEL), lambda i, meta: (i // per_b, 0, 0)),
                      pl.BlockSpec((1, D_MODEL), lambda i, meta: (0, 0)),
                      pl.BlockSpec(memory_space=pl.ANY)],
            out_specs=tok,
            scratch_shapes=[pltpu.VMEM((2, SUBS, SORT_ROWS, D_MODEL), BF16), pltpu.SemaphoreType.DMA((2,))]),
        out_shape=jax.ShapeDtypeStruct((t, D_MODEL), F32),
        compiler_params=_params(("arbitrary",)),
        name="final",
    )(meta, x1, pos, gt2, gf, ys)


def kernel(x, c, w_ada, b_ada, norm1_g, w_in, w_gk2, b_gk, gla_norm_g, sink, w_o, norm2_g,
           w_group, b_group, w_router, b_router, w_gate, w_up, w_down, norm_f_g):
    bsz, seq, d = x.shape
    assert w_ada.shape[0] == 1, "single layer: the final norm is fused behind the layer's MoE"
    l = 0
    t = bsz * seq
    n_sub = t // MIX_SUB
    cap = -(-(t + n_sub * (ROW_PACK - 1) + TM_EXP) // TM_EXP) * TM_EXP
    n_work = t // TM_EXP + n_sub * (ROW_PACK - 1) * N_GROUPS // TM_EXP + N_GROUPS + 1

    mod = _mod_call(c, w_ada[l], b_ada[l])
    sh1, sc1, gt1, sh2, sc2, gt2 = [m.reshape(bsz, 1, d) for m in jnp.split(mod, N_MOD, axis=-1)]

    w_parts = _split_cols(w_in[l])
    w_parts[GATE] = jnp.pad(w_parts[GATE], ((0, 0), (0, LANES - GLA_GATE_RANK)))
    weights = [wt.astype(BF16) for wt in w_parts + [w_o[l]]]
    w2p = jnp.pad(w_gk2[l], ((0, LANES - GLA_GATE_RANK), (0, 0))).astype(BF16)
    fill = jnp.full((SWA_HEADS, 2 * WINDOW), -jnp.inf, F32).at[:, 0].set(sink[l])
    n_logit = N_GROUPS + N_EXPERTS
    wrt = jnp.zeros((n_logit, ROUTE_STRIDE, d), F32).at[:, 0, :].set(
        jnp.concatenate([w_group[l], w_router[l]], axis=1).T).reshape(n_logit * ROUTE_STRIDE, d).astype(BF16)
    brt = jnp.zeros((n_logit, ROUTE_STRIDE), F32).at[:, 0].set(
        jnp.concatenate([b_group[l], b_router[l]])).reshape(n_logit * ROUTE_STRIDE, 1)

    x1, hs, pos, meta, (wg, wu, wd) = _mixer_call(
        x.reshape(t, d), (norm1_g[l].reshape(1, d), sc1, sh1, gt1), (norm2_g[l].reshape(1, d), sc2, sh2),
        w2p, b_gk[l].reshape(1, GLA_KEY_W), gla_norm_g[l].reshape(1, GLA_DV), fill, wrt, brt, weights,
        (w_gate[l], w_up[l], w_down[l]), cap, bsz, seq)

    last = meta.reshape(n_sub, META)[-1]
    totals = last[:N_GROUPS] + last[N_GROUPS:]
    ends = jnp.cumsum((totals + TM_EXP - 1) // TM_EXP)
    item = jnp.minimum(jnp.arange(n_work, dtype=jnp.int32), ends[-1] - 1)
    grp = jnp.sum(item[:, None] >= ends[None, :], axis=1).astype(jnp.int32)
    blk = item - jnp.concatenate([jnp.zeros((1,), ends.dtype), ends[:-1]])[grp]
    valid = (jnp.arange(n_work) < ends[-1]).astype(jnp.int32)

    ys = _experts_call(grp, blk.astype(jnp.int32), valid, hs, wg, wu, wd)
    out = _final_call(meta, x1, pos, gt2, norm_f_g.reshape(1, d), ys, seq)
    return out.reshape(bsz, seq, d)


def _split_cols(w):
    parts, start = [], 0
    for width in IN_WIDTHS:
        parts.append(w[:, start:start + width])
        start += width
    return parts
```

```python
import functools

import jax
import jax.numpy as jnp
from jax import lax
from jax.experimental import pallas as pl
from jax.experimental.pallas import tpu as pltpu

F32 = jnp.float32
BF16 = jnp.bfloat16

D_MODEL = 1024
GLA_HEADS = 4
GLA_DK = 128
GLA_DV = 256
GLA_KEY_W = GLA_HEADS * GLA_DK
GLA_VAL_W = GLA_HEADS * GLA_DV
GLA_GATE_RANK = 16
GLA_TAU = 16.0
GLA_CHUNK = 64
SWA_HEADS = 16
SWA_KV_HEADS = 2
SWA_GROUP = SWA_HEADS // SWA_KV_HEADS
SWA_DH = 64
SWA_W = SWA_HEADS * SWA_DH
SWA_KV_W = SWA_KV_HEADS * SWA_DH
WINDOW = 128
N_GROUPS = 4
EXPERTS_PER_GROUP = 4
N_EXPERTS = N_GROUPS * EXPERTS_PER_GROUP
D_EXPERT = 256
N_MOD = 6
EPS = 1e-6
IN_WIDTHS = (GLA_KEY_W, GLA_KEY_W, GLA_VAL_W, GLA_VAL_W, GLA_GATE_RANK,
             SWA_W, SWA_KV_W, SWA_KV_W, D_MODEL, D_MODEL)

LANES = 128
LOG2_E = 1.4426950408889634
LN_2 = 0.6931471805599453
VMEM_LIMIT = 56 * 1024 * 1024

TM_MIX = 512
MIX_SUB = 256
SUBS = TM_MIX // MIX_SUB
ROW_PACK = 16
SORT_ROWS = MIX_SUB + 64
SEG_BITS = (256, 128, 64, 32, 16)
HS_W = D_MODEL + LANES
TM_EXP = 512
META = 2 * N_GROUPS
ROUTE_STRIDE = 8
TN_MOD = 512


def _params(sem):
    return pltpu.CompilerParams(dimension_semantics=sem, vmem_limit_bytes=VMEM_LIMIT)


def _split(a):
    hi = a.astype(BF16)
    lo = (a - hi.astype(F32)).astype(BF16)
    return hi, lo


def _dot(a, b):
    return jnp.dot(a, b, preferred_element_type=F32)


def _dot_nt(a, b):
    return lax.dot_general(a, b, (((1,), (1,)), ((), ())), preferred_element_type=F32)


def _dot_tn(a, b):
    return lax.dot_general(a, b, (((0,), (0,)), ((), ())), preferred_element_type=F32)


def _dot3(a, b):
    a_hi, a_lo = _split(a)
    b_hi, b_lo = _split(b)
    return _dot(a_hi, b_hi) + _dot(a_hi, b_lo) + _dot(a_lo, b_hi)


def _sigmoid(x):
    return 1.0 / (1.0 + jnp.exp(-x))


def _silu(x):
    return x * _sigmoid(x)


def _rms(x, g):
    return x * lax.rsqrt(jnp.mean(x * x, axis=-1, keepdims=True) + EPS) * g


def _first_of(vals, target):
    idx = jnp.full(target.shape, len(vals) - 1, jnp.int32)
    for j in range(len(vals) - 2, -1, -1):
        idx = jnp.where(vals[j] == target, j, idx)
    return idx


def _rows_to_cols(a):
    return jnp.concatenate([jnp.transpose(a[:, i * LANES:(i + 1) * LANES]) for i in range(a.shape[1] // LANES)],
                           axis=0)


def _segment_copies(pcs, src_rows, dst_rows, make_copy, action):
    for g in range(N_GROUPS):
        for bit in SEG_BITS:
            done = pcs[g] & ~(2 * bit - 1)

            @pl.when((pcs[g] & bit) != 0)
            def _(g=g, bit=bit, done=done):
                cp = make_copy(g, pl.multiple_of(src_rows[g] + done, ROW_PACK),
                               pl.multiple_of(dst_rows[g] + done, ROW_PACK), bit)
                cp.start() if action == "start" else cp.wait()


def _mod_kernel(c_ref, w_ref, b_ref, o_ref):
    o_ref[...] = _dot3(_silu(c_ref[...]), w_ref[...]) + b_ref[...]


def _mod_call(c, w_ada, b_ada):
    bsz = c.shape[0]
    n = w_ada.shape[1]
    return pl.pallas_call(
        _mod_kernel,
        grid=(n // TN_MOD,),
        in_specs=[pl.BlockSpec((bsz, D_MODEL), lambda j: (0, 0)),
                  pl.BlockSpec((D_MODEL, TN_MOD), lambda j: (0, j)),
                  pl.BlockSpec((1, TN_MOD), lambda j: (0, j))],
        out_specs=pl.BlockSpec((bsz, TN_MOD), lambda j: (0, j)),
        out_shape=jax.ShapeDtypeStruct((bsz, n), F32),
        compiler_params=_params(("parallel",)),
        name="mod",
    )(c, w_ada, b_ada.reshape(1, n))


QA, KA, VA, RA, GATE, QB, KB, VB, GA, GB, WO = range(11)
N_WEIGHTS = 11
N_VMEM_IN = 18


def _mixer_kernel(x_ref, g1_ref, sc1_ref, sh1_ref, gt1_ref, w2_ref, bgk_ref, tri_ref, gn_ref, fill_ref,
                  g2_ref, sc2_ref, sh2_ref, wrt_ref, brt_ref, upper_ref, wgf_ref, wuf_ref, wdf_ref, *rest):
    w_hbm = rest[:N_WEIGHTS]
    x1_ref, hs_hbm, pos_hbm, meta_ref, wgb_ref, wub_ref, wdb_ref = rest[N_WEIGHTS:N_WEIGHTS + 7]
    w = rest[N_WEIGHTS + 7:2 * N_WEIGHTS + 7]
    (st_ref, kprev_ref, vprev_ref, oa_ref, ob_ref, stage_ref, pstage_ref, hkeep_ref, zero_ref, base_ref,
     sem) = rest[2 * N_WEIGHTS + 7:]
    first = pl.program_id(1) == 0
    step = pl.program_id(0) * pl.num_programs(1) + pl.program_id(1)
    n_steps = pl.num_programs(0) * pl.num_programs(1)

    @pl.when(step == 0)
    def _():
        for src, dst in zip(w_hbm, w):
            pltpu.sync_copy(src, dst)
        for g in range(N_GROUPS):
            base_ref[g] = 0
        hkeep_ref[...] = jnp.zeros_like(hkeep_ref)

    @pl.when(first)
    def _():
        st_ref[...] = jnp.zeros_like(st_ref)
        kprev_ref[...] = jnp.zeros_like(kprev_ref)
        vprev_ref[...] = jnp.zeros_like(vprev_ref)

    wgb_ref[...] = wgf_ref[...].astype(BF16)
    wub_ref[...] = wuf_ref[...].astype(BF16)
    wdb_ref[...] = wdf_ref[...].astype(BF16)

    def stage_copy(slot):
        return lambda g, src_row, dst_row, size: pltpu.make_async_copy(
            stage_ref.at[slot, pl.ds(src_row, size), :], hs_hbm.at[g, pl.ds(dst_row, size), :], sem.at[slot])

    pos_copy = lambda slot, tile: pltpu.make_async_copy(pstage_ref.at[slot], pos_hbm.at[tile], sem.at[slot])
    zeros4 = [0] * N_GROUPS

    def publish(sub, owner, routed, drain):
        sorted_rows, pos, pcs, offs = routed
        tile = owner * SUBS + sub
        m0 = tile * META

        @pl.when(owner > 0)
        def _():
            prev = [meta_ref[m0 - SUBS * META + N_GROUPS + g] for g in range(N_GROUPS)]
            _segment_copies(prev, zeros4, zeros4, stage_copy(sub), "wait")
            pos_copy(sub, tile).wait()

        stage_ref[sub] = sorted_rows
        pstage_ref[sub] = jnp.broadcast_to(pos, (8, MIX_SUB))
        bases = [base_ref[g] for g in range(N_GROUPS)]
        for g in range(N_GROUPS):
            meta_ref[m0 + g] = bases[g]
            meta_ref[m0 + N_GROUPS + g] = pcs[g]
            base_ref[g] = bases[g] + pcs[g]
        _segment_copies(pcs, offs, bases, stage_copy(sub), "start")
        pos_copy(sub, tile).start()

        def wait_own():
            _segment_copies(pcs, zeros4, zeros4, stage_copy(sub), "wait")
            pos_copy(sub, tile).wait()

        if drain is True:
            wait_own()
        elif drain is not False:
            pl.when(drain)(wait_own)

    route = lambda hb: _route_and_sort(hb, wrt_ref, brt_ref, upper_ref)
    moe_in = lambda x1: (_rms(x1, g2_ref[...]) * (1.0 + sc2_ref[...]) + sh2_ref[...]).astype(BF16)
    last_step = step == n_steps - 1

    routing, results = route(hkeep_ref[...]), []
    for sub in range(SUBS):
        r0 = sub * MIX_SUB
        x1, routed = _mixer_tile(x_ref[r0:r0 + MIX_SUB, :], first if sub == 0 else False, g1_ref, sc1_ref,
                                 sh1_ref, gt1_ref, w2_ref, bgk_ref, tri_ref, gn_ref, fill_ref, w, st_ref,
                                 kprev_ref, vprev_ref, oa_ref, ob_ref, side=routing)
        x1_ref[r0:r0 + MIX_SUB, :] = x1
        results.append(routed)
        if sub + 1 < SUBS:
            routing = route(moe_in(x1))
        else:
            hkeep_ref[...] = moe_in(x1)

    @pl.when(step > 0)
    def _():
        publish(SUBS - 1, step - 1, results[0], False)

    for sub in range(1, SUBS):
        publish(sub - 1, step, results[sub], last_step)

    @pl.when(last_step)
    def _():
        routing = route(hkeep_ref[...])
        next(routing)
        next(routing)
        publish(SUBS - 1, step, next(routing), True)
        zero_ref[...] = jnp.zeros_like(zero_ref)
        tails = [pltpu.make_async_copy(zero_ref, hs_hbm.at[g, pl.ds(pl.multiple_of(base_ref[g], ROW_PACK), TM_EXP), :],
                                       sem.at[0]) for g in range(N_GROUPS)]
        for cp in tails:
            cp.start()
        for cp in tails:
            cp.wait()


def _mixer_tile(x, first, g1_ref, sc_ref, sh_ref, gt_ref, w2_ref, bgk_ref, tri_ref, gn_ref, fill_ref,
                w, st_ref, kprev_ref, vprev_ref, oa_ref, ob_ref, side=None):
    ts = MIX_SUB
    advance = (lambda: next(side)) if side is not None else (lambda: None)
    hb = (_rms(x, g1_ref[...]) * (1.0 + sc_ref[...]) + sh_ref[...]).astype(BF16)

    z = _dot(hb, w[GATE][...])
    gz = _dot(z.astype(BF16), w2_ref[...]) + bgk_ref[...]
    soft = jnp.log2(1.0 + jnp.exp2(jnp.abs(gz) * (-LOG2_E)))
    la_hi, la_lo = _split(jnp.minimum(gz, 0.0) * (1.0 / GLA_TAU) - soft * (LN_2 / GLA_TAU))
    qa = _dot(hb, w[QA][...])
    ka = _dot(hb, w[KA][...])
    advance()
    va = _dot(hb, w[VA][...]).astype(BF16)
    bc = _dot(tri_ref[...], la_hi) + _dot(tri_ref[...], la_lo)
    ra = _dot(hb, w[RA][...])
    advance()
    qb = _dot(hb, w[QB][...]).astype(BF16) * jnp.asarray(SWA_DH ** -0.5, BF16)
    kb = _dot(hb, w[KB][...]).astype(BF16)
    vb = _dot(hb, w[VB][...]).astype(BF16)

    nc = ts // GLA_CHUNK
    chunk = lambda c: slice(c * GLA_CHUNK, (c + 1) * GLA_CHUNK)
    btot = [bc[(c + 1) * GLA_CHUNK - 1:(c + 1) * GLA_CHUNK, :] for c in range(nc)]
    pre = [jnp.zeros_like(btot[0])]
    for c in range(nc):
        pre.append(pre[c] + btot[c])
    rows = lambda vecs: jnp.concatenate([jnp.broadcast_to(v, (GLA_CHUNK, GLA_KEY_W)) for v in vecs], axis=0)
    qf = qa * (GLA_DK ** -0.5) * jnp.exp(bc)
    q_dec = qf.astype(BF16)
    q_in = (qf * rows([jnp.exp(pre[c]) for c in range(nc)])).astype(BF16)
    k_inv = (ka * jnp.exp(-bc)).astype(BF16)
    k_end = ka * jnp.exp(rows(btot) - bc)
    k_tile = (k_end * rows([jnp.exp(pre[nc] - pre[c + 1]) for c in range(nc)])).astype(BF16)
    k_cross = {(c, c2): (k_end[chunk(c2)] * jnp.exp(pre[c] - pre[c2 + 1])).astype(BF16)
               for c in range(nc) for c2 in range(c)}
    dec_tile = jnp.exp(pre[nc])
    causal = (lax.broadcasted_iota(jnp.int32, (ts, ts), 1) <= lax.broadcasted_iota(jnp.int32, (ts, ts), 0))

    def gla_scores(h):
        ks = slice(h * GLA_DK, (h + 1) * GLA_DK)
        att = []
        for c in range(nc):
            keys = [k_cross[(c, c2)][:, ks] for c2 in range(c)] + [k_inv[chunk(c), ks]]
            keys += [jnp.zeros((GLA_CHUNK, GLA_DK), BF16)] * (nc - 1 - c)
            att.append(_dot_nt(q_dec[chunk(c), ks], jnp.concatenate(keys, axis=0)))
        return jnp.where(causal, jnp.concatenate(att, axis=0), 0.0).astype(BF16)

    atts = [gla_scores(h) for h in range(GLA_HEADS)]
    gn = gn_ref[...]
    for h in range(GLA_HEADS):
        ks = slice(h * GLA_DK, (h + 1) * GLA_DK)
        vs = slice(h * GLA_DV, (h + 1) * GLA_DV)
        st = st_ref[h]
        o = _dot(atts[h], va[:, vs]) + _dot(q_in[:, ks], st.astype(BF16))
        dec_col = jnp.transpose(jnp.broadcast_to(dec_tile[:, ks], (8, GLA_DK)))[:, 0:1]
        st_ref[h] = st * dec_col + _dot_tn(k_tile[:, ks], va[:, vs])
        o = o * lax.rsqrt(jnp.mean(o * o, axis=-1, keepdims=True) + EPS) * gn
        oa_ref[:, vs] = o * _silu(ra[:, vs])

    wd = WINDOW
    k_all = jnp.concatenate([kprev_ref[...], kb], axis=0)
    v_all = jnp.concatenate([vprev_ref[...], vb], axis=0)
    kprev_ref[...] = kb[ts - wd:, :]
    vprev_ref[...] = vb[ts - wd:, :]
    lo_half = lax.broadcasted_iota(jnp.int32, (wd + ts, LANES), 1) < SWA_DH
    k_sw = pltpu.roll(k_all, SWA_DH, 1)
    v_sw = pltpu.roll(v_all, SWA_DH, 1)
    k2 = (jnp.where(lo_half, k_all, k_sw), jnp.where(lo_half, k_sw, k_all))
    v2 = (jnp.where(lo_half, v_all, v_sw), jnp.where(lo_half, v_sw, v_all))
    qi = lax.broadcasted_iota(jnp.int32, (wd, 2 * wd), 0) + wd
    kj = lax.broadcasted_iota(jnp.int32, (wd, 2 * wd), 1)
    band = (qi - kj >= 0) & (qi - kj < wd)
    band_first = band & (jnp.logical_not(first) | (kj >= wd)) if first is not False else band
    q_lo = lax.broadcasted_iota(jnp.int32, (wd, LANES), 1) < SWA_DH
    row0 = lax.broadcasted_iota(jnp.int32, (2 * wd, LANES), 0) == 0
    ones = jnp.ones((2 * wd, LANES), BF16)
    half = SWA_GROUP // 2
    combos = [(blk, hk) for blk in range(ts // wd) for hk in range(SWA_KV_HEADS)]

    def swa_probs(blk, hk):
        parts = []
        for p in range(half):
            qp = qb[blk * wd:(blk + 1) * wd, (hk * half + p) * LANES:(hk * half + p + 1) * LANES]
            parts += [jnp.where(q_lo, qp, jnp.zeros_like(qp)), jnp.where(q_lo, jnp.zeros_like(qp), qp)]
        s = _dot_nt(jnp.concatenate(parts, axis=0), k2[hk][blk * wd:blk * wd + 2 * wd])
        valid = band_first if blk == 0 else band
        es = []
        for r in range(SWA_GROUP):
            head = hk * SWA_GROUP + r
            sh = jnp.where(valid, s[r * wd:(r + 1) * wd], fill_ref[head:head + 1, :])
            es.append(jnp.exp(sh - jnp.max(sh, axis=-1, keepdims=True)).astype(BF16))
        return jnp.concatenate(es, axis=0)

    probs = [swa_probs(blk, hk) for blk, hk in combos]
    side_out = advance()
    ga = _dot(hb, w[GA][...])
    gb = _dot(hb, w[GB][...])
    for (blk, hk), e in zip(combos, probs):
        vv = jnp.where(row0, jnp.zeros_like(ones), v2[hk][blk * wd:blk * wd + 2 * wd])
        o = _dot(e, jnp.concatenate([vv, ones], axis=1))
        for p in range(half):
            oe = o[(2 * p) * wd:(2 * p + 1) * wd]
            oo = o[(2 * p + 1) * wd:(2 * p + 2) * wd]
            num = jnp.where(q_lo, oe[:, :LANES], oo[:, :LANES])
            den = jnp.where(q_lo, oe[:, LANES:], oo[:, LANES:])
            ob_ref[blk * wd:(blk + 1) * wd, (hk * half + p) * LANES:(hk * half + p + 1) * LANES] = num / den

    merged = _sigmoid(ga) * oa_ref[...] + _sigmoid(gb) * ob_ref[...]
    return x + gt_ref[...] * _dot(merged.astype(BF16), w[WO][...]), side_out


def _route_and_sort(hb, wrt_ref, brt_ref, upper_ref):
    tm = hb.shape[0]
    neg = -jnp.inf
    lt = _dot_nt(wrt_ref[...], hb) + brt_ref[...]
    yield None
    lrow = lambda k: lt[ROUTE_STRIDE * k:ROUTE_STRIDE * k + 1, :]
    lg = [lrow(g) for g in range(N_GROUPS)]
    gmax = functools.reduce(jnp.maximum, lg)
    g_w = 1.0 / functools.reduce(jnp.add, [jnp.exp(v - gmax) for v in lg])
    g_idx = _first_of(lg, gmax)
    le = []
    for j in range(EXPERTS_PER_GROUP):
        v = lrow(N_GROUPS + (N_GROUPS - 1) * EXPERTS_PER_GROUP + j)
        for g in range(N_GROUPS - 2, -1, -1):
            v = jnp.where(g_idx == g, lrow(N_GROUPS + g * EXPERTS_PER_GROUP + j), v)
        le.append(v)
    v1 = functools.reduce(jnp.maximum, le)
    i1 = _first_of(le, v1)
    le2 = [jnp.where(i1 == j, neg, le[j]) for j in range(EXPERTS_PER_GROUP)]
    v2 = functools.reduce(jnp.maximum, le2)
    i2 = _first_of(le2, v2)
    e2 = jnp.exp(v2 - v1)
    w1 = g_w / (1.0 + e2)
    w2 = g_w * e2 / (1.0 + e2)

    sub = lax.broadcasted_iota(jnp.int32, (ROW_PACK, tm), 0)
    rank = _dot((sub == g_idx).astype(BF16), upper_ref[...])
    yield None
    pcs, offs, run = [], [], 0
    pos = jnp.zeros((1, tm), F32)
    for g in range(N_GROUPS):
        mine = g_idx == g
        cnt = jnp.sum(mine.astype(F32), axis=1, keepdims=True)[0, 0].astype(jnp.int32)
        pcs.append(((cnt + (ROW_PACK - 1)) // ROW_PACK) * ROW_PACK)
        offs.append(run)
        pos = pos + jnp.where(mine, rank[g:g + 1, :] + jnp.asarray(run, F32), 0.0)
        run = run + pcs[g]
    perm = (lax.broadcasted_iota(jnp.int32, (SORT_ROWS, tm), 0) == pos.astype(jnp.int32)).astype(BF16)
    cw = jnp.where(sub == i1, w1, 0.0) + jnp.where(sub == i2, w2, 0.0)
    cw_hi = cw.astype(BF16).astype(F32)
    packed = cw_hi + pltpu.roll(cw - cw_hi, EXPERTS_PER_GROUP, 0)
    cw_cols = _rows_to_cols(jnp.concatenate([packed, jnp.zeros((LANES - ROW_PACK, tm), F32)], axis=0))
    sorted_rows = _dot(perm, jnp.concatenate([hb, cw_cols.astype(BF16)], axis=1)).astype(BF16)
    yield sorted_rows, pos, pcs, offs


def _mixer_call(x2, vecs1, vecs2, w2p, bgk, gn, fill, wrt, brt, weights, experts_f32, cap, bsz, seq):
    g1, sc1, sh1, gt1 = vecs1
    g2, sc2, sh2 = vecs2
    ns = seq // TM_MIX
    n_steps = bsz * ns
    tok = pl.BlockSpec((TM_MIX, D_MODEL), lambda b, s: (b * ns + s, 0))
    vec = pl.BlockSpec((None, 1, D_MODEL), lambda b, s: (b, 0, 0))
    full = lambda a: pl.BlockSpec(a.shape, lambda b, s: (0,) * a.ndim)
    hbm = pl.BlockSpec(memory_space=pl.ANY)
    slab = lambda a: pl.BlockSpec((a.shape[0] // n_steps, a.shape[1]), lambda b, s: (b * ns + s, 0))
    idx = jnp.arange(MIX_SUB)
    tri = ((idx[:, None] // GLA_CHUNK == idx[None, :] // GLA_CHUNK) & (idx[None, :] <= idx[:, None])).astype(BF16)
    upper = (idx[:, None] < idx[None, :]).astype(BF16)
    ex2d = [e.reshape(-1, e.shape[-1]) for e in experts_f32]
    n_sub = n_steps * SUBS
    vmem_in = [x2, g1, sc1, sh1, gt1, w2p, bgk, tri, gn, fill, g2, sc2, sh2, wrt, brt, upper]
    assert len(vmem_in) + len(ex2d) == N_VMEM_IN + 1
    outs = pl.pallas_call(
        _mixer_kernel,
        grid=(bsz, ns),
        in_specs=[tok, full(g1), vec, vec, vec, full(w2p), full(bgk), full(tri), full(gn), full(fill),
                  full(g2), vec, vec, full(wrt), full(brt), full(upper)]
        + [slab(e) for e in ex2d] + [hbm] * N_WEIGHTS,
        out_specs=[tok, hbm, hbm, pl.BlockSpec(memory_space=pltpu.SMEM)] + [slab(e) for e in ex2d],
        out_shape=[jax.ShapeDtypeStruct(x2.shape, F32),
                   jax.ShapeDtypeStruct((N_GROUPS, cap, HS_W), BF16),
                   jax.ShapeDtypeStruct((n_sub, 8, MIX_SUB), F32),
                   jax.ShapeDtypeStruct((n_sub * META,), jnp.int32)]
        + [jax.ShapeDtypeStruct(e.shape, BF16) for e in ex2d],
        scratch_shapes=[pltpu.VMEM(wt.shape, BF16) for wt in weights]
        + [pltpu.VMEM((GLA_HEADS, GLA_DK, GLA_DV), F32),
           pltpu.VMEM((WINDOW, SWA_KV_W), BF16), pltpu.VMEM((WINDOW, SWA_KV_W), BF16),
           pltpu.VMEM((MIX_SUB, GLA_VAL_W), F32), pltpu.VMEM((MIX_SUB, SWA_W), F32),
           pltpu.VMEM((SUBS, SORT_ROWS, HS_W), BF16), pltpu.VMEM((SUBS, 8, MIX_SUB), F32),
           pltpu.VMEM((MIX_SUB, D_MODEL), BF16), pltpu.VMEM((TM_EXP, HS_W), BF16),
           pltpu.SMEM((N_GROUPS,), jnp.int32), pltpu.SemaphoreType.DMA((SUBS,))],
        compiler_params=_params(("arbitrary", "arbitrary")),
        name="mixer",
    )(*vmem_in, *ex2d, *weights)
    x1, hs, pos, meta = outs[:4]
    experts_bf16 = [o.reshape(e.shape) for o, e in zip(outs[4:], experts_f32)]
    return x1, hs, pos, meta, experts_bf16


def _experts_kernel(grp_ref, blk_ref, valid_ref, hs_ref, wg_ref, wu_ref, wd_ref, ys_ref):
    @pl.when(valid_ref[pl.program_id(0)] != 0)
    def _():
        hx = hs_ref[...]
        h = hx[:, :D_MODEL]
        cw = hx[:, D_MODEL:].astype(F32)
        gates = [_dot(h, wg_ref[j]) for j in range(EXPERTS_PER_GROUP)]
        ups = [_dot(h, wu_ref[j]) for j in range(EXPERTS_PER_GROUP)]
        hids = [(_silu(gates[j]) * ups[j]
                 * (cw[:, j:j + 1] + cw[:, EXPERTS_PER_GROUP + j:EXPERTS_PER_GROUP + j + 1])).astype(BF16)
                for j in range(EXPERTS_PER_GROUP)]
        y = _dot(hids[0], wd_ref[0])
        for j in range(1, EXPERTS_PER_GROUP):
            y = y + _dot(hids[j], wd_ref[j])
        ys_ref[...] = y.astype(BF16)


def _experts_call(grp, blk, valid, hs, wg, wu, wd):
    n_work = grp.shape[0]
    rows = lambda width: pl.BlockSpec((None, TM_EXP, width), lambda i, grp, blk, valid: (grp[i], blk[i], 0))
    wspec = lambda a: pl.BlockSpec((EXPERTS_PER_GROUP,) + a.shape[1:], lambda i, grp, blk, valid: (grp[i], 0, 0))
    return pl.pallas_call(
        _experts_kernel,
        grid_spec=pltpu.PrefetchScalarGridSpec(
            num_scalar_prefetch=3,
            grid=(n_work,),
            in_specs=[rows(HS_W), wspec(wg), wspec(wu), wspec(wd)],
            out_specs=rows(D_MODEL)),
        out_shape=jax.ShapeDtypeStruct(hs.shape[:2] + (D_MODEL,), BF16),
        compiler_params=_params(("arbitrary",)),
        name="experts",
    )(grp, blk, valid, hs, wg, wu, wd)


def _final_kernel(meta_ref, x1_ref, pos_ref, gt_ref, gf_ref, ys_hbm, o_ref, ybuf_ref, sem):
    step = pl.program_id(0)

    def fetch(s, action):
        half = s % 2
        for sub in range(SUBS):
            m0 = (s * SUBS + sub) * META
            bases = [meta_ref[m0 + g] for g in range(N_GROUPS)]
            pcs = [meta_ref[m0 + N_GROUPS + g] for g in range(N_GROUPS)]
            offs, run = [], 0
            for g in range(N_GROUPS):
                offs.append(run)
                run = run + pcs[g]
            _segment_copies(pcs, bases, offs, lambda g, src_row, dst_row, size, sub=sub: pltpu.make_async_copy(
                ys_hbm.at[g, pl.ds(src_row, size), :], ybuf_ref.at[half, sub, pl.ds(dst_row, size), :],
                sem.at[half]), action)

    @pl.when(step == 0)
    def _():
        ybuf_ref[...] = jnp.zeros_like(ybuf_ref)
        fetch(step, "start")

    @pl.when(step + 1 < pl.num_programs(0))
    def _():
        fetch(step + 1, "start")

    fetch(step, "wait")
    half = step % 2
    for sub in range(SUBS):
        r0 = sub * MIX_SUB
        pos_col = _rows_to_cols(pos_ref[sub])[:, 0:1].astype(jnp.int32)
        sel = (lax.broadcasted_iota(jnp.int32, (MIX_SUB, SORT_ROWS), 1) == pos_col).astype(BF16)
        y = _dot(sel, ybuf_ref[half, sub])
        o_ref[r0:r0 + MIX_SUB, :] = _rms(x1_ref[r0:r0 + MIX_SUB, :] + gt_ref[...] * y, gf_ref[...])


def _final_call(meta, x1, pos, gt2, gf, ys, seq):
    t = x1.shape[0]
    per_b = seq // TM_MIX
    tok = pl.BlockSpec((TM_MIX, D_MODEL), lambda i, meta: (i, 0))
    return pl.pallas_call(
        _final_kernel,
        grid_spec=pltpu.PrefetchScalarGridSpec(
            num_scalar_prefetch=1,
            grid=(t // TM_MIX,),
            in_specs=[tok, pl.BlockSpec((SUBS, 8, MIX_SUB), lambda i, meta: (i, 0, 0)),
                      pl.BlockSpec((None, 1, D_MODEL), lambda i, meta: (i // per_b, 0, 0)),
                      pl.BlockSpec((1, D_MODEL), lambda i, meta: (0, 0)),
                      pl.BlockSpec(memory_space=pl.ANY)],
            out_specs=tok,
            scratch_shapes=[pltpu.VMEM((2, SUBS, SORT_ROWS, D_MODEL), BF16), pltpu.SemaphoreType.DMA((2,))]),
        out_shape=jax.ShapeDtypeStruct((t, D_MODEL), F32),
        compiler_params=_params(("arbitrary",)),
        name="final",
    )(meta, x1, pos, gt2, gf, ys)


def kernel(x, c, w_ada, b_ada, norm1_g, w_in, w_gk2, b_gk, gla_norm_g, sink, w_o, norm2_g,
           w_group, b_group, w_router, b_router, w_gate, w_up, w_down, norm_f_g):
    bsz, seq, d = x.shape
    assert w_ada.shape[0] == 1, "single layer: the final norm is fused behind the layer's MoE"
    l = 0
    t = bsz * seq
    n_sub = t // MIX_SUB
    cap = -(-(t + n_sub * (ROW_PACK - 1) + TM_EXP) // TM_EXP) * TM_EXP
    n_work = t // TM_EXP + n_sub * (ROW_PACK - 1) * N_GROUPS // TM_EXP + N_GROUPS + 1

    mod = _mod_call(c, w_ada[l], b_ada[l])
    sh1, sc1, gt1, sh2, sc2, gt2 = [m.reshape(bsz, 1, d) for m in jnp.split(mod, N_MOD, axis=-1)]

    w_parts = _split_cols(w_in[l])
    w_parts[GATE] = jnp.pad(w_parts[GATE], ((0, 0), (0, LANES - GLA_GATE_RANK)))
    weights = [wt.astype(BF16) for wt in w_parts + [w_o[l]]]
    w2p = jnp.pad(w_gk2[l], ((0, LANES - GLA_GATE_RANK), (0, 0))).astype(BF16)
    fill = jnp.full((SWA_HEADS, 2 * WINDOW), -jnp.inf, F32).at[:, 0].set(sink[l])
    n_logit = N_GROUPS + N_EXPERTS
    wrt = jnp.zeros((n_logit, ROUTE_STRIDE, d), F32).at[:, 0, :].set(
        jnp.concatenate([w_group[l], w_router[l]], axis=1).T).reshape(n_logit * ROUTE_STRIDE, d).astype(BF16)
    brt = jnp.zeros((n_logit, ROUTE_STRIDE), F32).at[:, 0].set(
        jnp.concatenate([b_group[l], b_router[l]])).reshape(n_logit * ROUTE_STRIDE, 1)

    x1, hs, pos, meta, (wg, wu, wd) = _mixer_call(
        x.reshape(t, d), (norm1_g[l].reshape(1, d), sc1, sh1, gt1), (norm2_g[l].reshape(1, d), sc2, sh2),
        w2p, b_gk[l].reshape(1, GLA_KEY_W), gla_norm_g[l].reshape(1, GLA_DV), fill, wrt, brt, weights,
        (w_gate[l], w_up[l], w_down[l]), cap, bsz, seq)

    last = meta.reshape(n_sub, META)[-1]
    totals = last[:N_GROUPS] + last[N_GROUPS:]
    ends = jnp.cumsum((totals + TM_EXP - 1) // TM_EXP)
    item = jnp.minimum(jnp.arange(n_work, dtype=jnp.int32), ends[-1] - 1)
    grp = jnp.sum(item[:, None] >= ends[None, :], axis=1).astype(jnp.int32)
    blk = item - jnp.concatenate([jnp.zeros((1,), ends.dtype), ends[:-1]])[grp]
    valid = (jnp.arange(n_work) < ends[-1]).astype(jnp.int32)

    ys = _experts_call(grp, blk.astype(jnp.int32), valid, hs, wg, wu, wd)
    out = _final_call(meta, x1, pos, gt2, norm_f_g.reshape(1, d), ys, seq)
    return out.reshape(bsz, seq, d)


def _split_cols(w):
    parts, start = [], 0
    for width in IN_WIDTHS:
        parts.append(w[:, start:start + width])
        start += width
    return parts
```

```python
import functools

import jax
import jax.numpy as jnp
from jax import lax
from jax.experimental import pallas as pl
from jax.experimental.pallas import tpu as pltpu

F32 = jnp.float32
BF16 = jnp.bfloat16

D_MODEL = 1024
GLA_HEADS = 4
GLA_DK = 128
GLA_DV = 256
GLA_KEY_W = GLA_HEADS * GLA_DK
GLA_VAL_W = GLA_HEADS * GLA_DV
GLA_GATE_RANK = 16
GLA_TAU = 16.0
GLA_CHUNK = 64
SWA_HEADS = 16
SWA_KV_HEADS = 2
SWA_GROUP = SWA_HEADS // SWA_KV_HEADS
SWA_DH = 64
SWA_W = SWA_HEADS * SWA_DH
SWA_KV_W = SWA_KV_HEADS * SWA_DH
WINDOW = 128
N_GROUPS = 4
EXPERTS_PER_GROUP = 4
N_EXPERTS = N_GROUPS * EXPERTS_PER_GROUP
D_EXPERT = 256
N_MOD = 6
EPS = 1e-6
IN_WIDTHS = (GLA_KEY_W, GLA_KEY_W, GLA_VAL_W, GLA_VAL_W, GLA_GATE_RANK,
             SWA_W, SWA_KV_W, SWA_KV_W, D_MODEL, D_MODEL)

LANES = 128
LOG2_E = 1.4426950408889634
LN_2 = 0.6931471805599453
VMEM_LIMIT = 56 * 1024 * 1024

TM_MIX = 512
MIX_SUB = 256
SUBS = TM_MIX // MIX_SUB
TM_FIN = 1024
FIN_SUBS = TM_FIN // MIX_SUB
ROW_PACK = 16
SORT_ROWS = MIX_SUB + 64
SEG_BITS = (256, 128, 64, 32, 16)
HS_W = D_MODEL + LANES
TM_EXP = 512
META = 2 * N_GROUPS
ROUTE_STRIDE = 8
TN_MOD = 512


def _params(sem):
    return pltpu.CompilerParams(dimension_semantics=sem, vmem_limit_bytes=VMEM_LIMIT)


def _split(a):
    hi = a.astype(BF16)
    lo = (a - hi.astype(F32)).astype(BF16)
    return hi, lo


def _dot(a, b):
    return jnp.dot(a, b, preferred_element_type=F32)


def _dot_nt(a, b):
    return lax.dot_general(a, b, (((1,), (1,)), ((), ())), preferred_element_type=F32)


def _dot_tn(a, b):
    return lax.dot_general(a, b, (((0,), (0,)), ((), ())), preferred_element_type=F32)


def _dot3(a, b):
    a_hi, a_lo = _split(a)
    b_hi, b_lo = _split(b)
    return _dot(a_hi, b_hi) + _dot(a_hi, b_lo) + _dot(a_lo, b_hi)


def _sigmoid(x):
    return 1.0 / (1.0 + jnp.exp(-x))


def _silu(x):
    return x * _sigmoid(x)


def _rms(x, g):
    return x * lax.rsqrt(jnp.mean(x * x, axis=-1, keepdims=True) + EPS) * g


def _first_of(vals, target):
    idx = jnp.full(target.shape, len(vals) - 1, jnp.int32)
    for j in range(len(vals) - 2, -1, -1):
        idx = jnp.where(vals[j] == target, j, idx)
    return idx


def _rows_to_cols(a):
    return jnp.concatenate([jnp.transpose(a[:, i * LANES:(i + 1) * LANES]) for i in range(a.shape[1] // LANES)],
                           axis=0)


def _segment_copies(pcs, src_rows, dst_rows, make_copy, action):
    for g in range(N_GROUPS):
        for bit in SEG_BITS:
            done = pcs[g] & ~(2 * bit - 1)

            @pl.when((pcs[g] & bit) != 0)
            def _(g=g, bit=bit, done=done):
                cp = make_copy(g, pl.multiple_of(src_rows[g] + done, ROW_PACK),
                               pl.multiple_of(dst_rows[g] + done, ROW_PACK), bit)
                cp.start() if action == "start" else cp.wait()


def _mod_kernel(c_ref, w_ref, b_ref, o_ref):
    o_ref[...] = _dot3(_silu(c_ref[...]), w_ref[...]) + b_ref[...]


def _mod_call(c, w_ada, b_ada):
    bsz = c.shape[0]
    n = w_ada.shape[1]
    return pl.pallas_call(
        _mod_kernel,
        grid=(n // TN_MOD,),
        in_specs=[pl.BlockSpec((bsz, D_MODEL), lambda j: (0, 0)),
                  pl.BlockSpec((D_MODEL, TN_MOD), lambda j: (0, j)),
                  pl.BlockSpec((1, TN_MOD), lambda j: (0, j))],
        out_specs=pl.BlockSpec((bsz, TN_MOD), lambda j: (0, j)),
        out_shape=jax.ShapeDtypeStruct((bsz, n), F32),
        compiler_params=_params(("parallel",)),
        name="mod",
    )(c, w_ada, b_ada.reshape(1, n))


QA, KA, VA, RA, GATE, QB, KB, VB, GA, GB, WO = range(11)
N_WEIGHTS = 11
N_VMEM_IN = 18


def _mixer_kernel(x_ref, g1_ref, sc1_ref, sh1_ref, gt1_ref, w2_ref, bgk_ref, tri_ref, gn_ref, fill_ref,
                  g2_ref, sc2_ref, sh2_ref, wrt_ref, brt_ref, upper_ref, wgf_ref, wuf_ref, wdf_ref, *rest):
    w_hbm = rest[:N_WEIGHTS]
    x1_ref, hs_hbm, pos_hbm, meta_ref, wgb_ref, wub_ref, wdb_ref = rest[N_WEIGHTS:N_WEIGHTS + 7]
    w = rest[N_WEIGHTS + 7:2 * N_WEIGHTS + 7]
    (st_ref, kprev_ref, vprev_ref, oa_ref, ob_ref, stage_ref, pstage_ref, hkeep_ref, zero_ref, base_ref,
     sem) = rest[2 * N_WEIGHTS + 7:]
    first = pl.program_id(1) == 0
    step = pl.program_id(0) * pl.num_programs(1) + pl.program_id(1)
    n_steps = pl.num_programs(0) * pl.num_programs(1)

    @pl.when(step == 0)
    def _():
        for src, dst in zip(w_hbm, w):
            pltpu.sync_copy(src, dst)
        for g in range(N_GROUPS):
            base_ref[g] = 0
        hkeep_ref[...] = jnp.zeros_like(hkeep_ref)

    @pl.when(first)
    def _():
        st_ref[...] = jnp.zeros_like(st_ref)
        kprev_ref[...] = jnp.zeros_like(kprev_ref)
        vprev_ref[...] = jnp.zeros_like(vprev_ref)

    wgb_ref[...] = wgf_ref[...].astype(BF16)
    wub_ref[...] = wuf_ref[...].astype(BF16)
    wdb_ref[...] = wdf_ref[...].astype(BF16)

    def stage_copy(slot):
        return lambda g, src_row, dst_row, size: pltpu.make_async_copy(
            stage_ref.at[slot, pl.ds(src_row, size), :], hs_hbm.at[g, pl.ds(dst_row, size), :], sem.at[slot])

    pos_copy = lambda slot, tile: pltpu.make_async_copy(pstage_ref.at[slot], pos_hbm.at[tile], sem.at[slot])
    zeros4 = [0] * N_GROUPS

    def publish(sub, owner, routed, drain):
        sorted_rows, pos, pcs, offs = routed
        tile = owner * SUBS + sub
        m0 = tile * META

        @pl.when(owner > 0)
        def _():
            prev = [meta_ref[m0 - SUBS * META + N_GROUPS + g] for g in range(N_GROUPS)]
            _segment_copies(prev, zeros4, zeros4, stage_copy(sub), "wait")
            pos_copy(sub, tile).wait()

        stage_ref[sub] = sorted_rows
        pstage_ref[sub] = jnp.broadcast_to(pos, (8, MIX_SUB))
        bases = [base_ref[g] for g in range(N_GROUPS)]
        for g in range(N_GROUPS):
            meta_ref[m0 + g] = bases[g]
            meta_ref[m0 + N_GROUPS + g] = pcs[g]
            base_ref[g] = bases[g] + pcs[g]
        _segment_copies(pcs, offs, bases, stage_copy(sub), "start")
        pos_copy(sub, tile).start()

        def wait_own():
            _segment_copies(pcs, zeros4, zeros4, stage_copy(sub), "wait")
            pos_copy(sub, tile).wait()

        if drain is True:
            wait_own()
        elif drain is not False:
            pl.when(drain)(wait_own)

    route = lambda hb: _route_and_sort(hb, wrt_ref, brt_ref, upper_ref)
    moe_in = lambda x1: (_rms(x1, g2_ref[...]) * (1.0 + sc2_ref[...]) + sh2_ref[...]).astype(BF16)
    last_step = step == n_steps - 1

    routing, results = route(hkeep_ref[...]), []
    for sub in range(SUBS):
        r0 = sub * MIX_SUB
        x1, routed = _mixer_tile(x_ref[r0:r0 + MIX_SUB, :], first if sub == 0 else False, g1_ref, sc1_ref,
                                 sh1_ref, gt1_ref, w2_ref, bgk_ref, tri_ref, gn_ref, fill_ref, w, st_ref,
                                 kprev_ref, vprev_ref, oa_ref, ob_ref, side=routing)
        x1_ref[r0:r0 + MIX_SUB, :] = x1
        results.append(routed)
        if sub + 1 < SUBS:
            routing = route(moe_in(x1))
        else:
            hkeep_ref[...] = moe_in(x1)

    @pl.when(step > 0)
    def _():
        publish(SUBS - 1, step - 1, results[0], False)

    for sub in range(1, SUBS):
        publish(sub - 1, step, results[sub], last_step)

    @pl.when(last_step)
    def _():
        routing = route(hkeep_ref[...])
        next(routing)
        next(routing)
        publish(SUBS - 1, step, next(routing), True)
        zero_ref[...] = jnp.zeros_like(zero_ref)
        tails = [pltpu.make_async_copy(zero_ref, hs_hbm.at[g, pl.ds(pl.multiple_of(base_ref[g], ROW_PACK), TM_EXP), :],
                                       sem.at[0]) for g in range(N_GROUPS)]
        for cp in tails:
            cp.start()
        for cp in tails:
            cp.wait()


def _mixer_tile(x, first, g1_ref, sc_ref, sh_ref, gt_ref, w2_ref, bgk_ref, tri_ref, gn_ref, fill_ref,
                w, st_ref, kprev_ref, vprev_ref, oa_ref, ob_ref, side=None):
    ts = MIX_SUB
    advance = (lambda: next(side)) if side is not None else (lambda: None)
    hb = (_rms(x, g1_ref[...]) * (1.0 + sc_ref[...]) + sh_ref[...]).astype(BF16)

    z = _dot(hb, w[GATE][...])
    gz = _dot(z.astype(BF16), w2_ref[...]) + bgk_ref[...]
    soft = jnp.log2(1.0 + jnp.exp2(jnp.abs(gz) * (-LOG2_E)))
    la_hi, la_lo = _split(jnp.minimum(gz, 0.0) * (1.0 / GLA_TAU) - soft * (LN_2 / GLA_TAU))
    qa = _dot(hb, w[QA][...])
    ka = _dot(hb, w[KA][...])
    advance()
    va = _dot(hb, w[VA][...]).astype(BF16)
    bc = _dot(tri_ref[...], la_hi) + _dot(tri_ref[...], la_lo)
    ra = _dot(hb, w[RA][...])
    advance()
    qb = _dot(hb, w[QB][...]).astype(BF16) * jnp.asarray(SWA_DH ** -0.5, BF16)
    kb = _dot(hb, w[KB][...]).astype(BF16)
    vb = _dot(hb, w[VB][...]).astype(BF16)

    nc = ts // GLA_CHUNK
    chunk = lambda c: slice(c * GLA_CHUNK, (c + 1) * GLA_CHUNK)
    btot = [bc[(c + 1) * GLA_CHUNK - 1:(c + 1) * GLA_CHUNK, :] for c in range(nc)]
    pre = [jnp.zeros_like(btot[0])]
    for c in range(nc):
        pre.append(pre[c] + btot[c])
    rows = lambda vecs: jnp.concatenate([jnp.broadcast_to(v, (GLA_CHUNK, GLA_KEY_W)) for v in vecs], axis=0)
    qf = qa * (GLA_DK ** -0.5) * jnp.exp(bc)
    q_dec = qf.astype(BF16)
    q_in = (qf * rows([jnp.exp(pre[c]) for c in range(nc)])).astype(BF16)
    k_inv = (ka * jnp.exp(-bc)).astype(BF16)
    k_end = ka * jnp.exp(rows(btot) - bc)
    k_tile = (k_end * rows([jnp.exp(pre[nc] - pre[c + 1]) for c in range(nc)])).astype(BF16)
    k_cross = {(c, c2): (k_end[chunk(c2)] * jnp.exp(pre[c] - pre[c2 + 1])).astype(BF16)
               for c in range(nc) for c2 in range(c)}
    dec_tile = jnp.exp(pre[nc])
    causal = (lax.broadcasted_iota(jnp.int32, (ts, ts), 1) <= lax.broadcasted_iota(jnp.int32, (ts, ts), 0))

    def gla_scores(h):
        ks = slice(h * GLA_DK, (h + 1) * GLA_DK)
        att = []
        for c in range(nc):
            keys = [k_cross[(c, c2)][:, ks] for c2 in range(c)] + [k_inv[chunk(c), ks]]
            keys += [jnp.zeros((GLA_CHUNK, GLA_DK), BF16)] * (nc - 1 - c)
            att.append(_dot_nt(q_dec[chunk(c), ks], jnp.concatenate(keys, axis=0)))
        return jnp.where(causal, jnp.concatenate(att, axis=0), 0.0).astype(BF16)

    atts = [gla_scores(h) for h in range(GLA_HEADS)]
    gn = gn_ref[...]
    for h in range(GLA_HEADS):
        ks = slice(h * GLA_DK, (h + 1) * GLA_DK)
        vs = slice(h * GLA_DV, (h + 1) * GLA_DV)
        st = st_ref[h]
        o = _dot(atts[h], va[:, vs]) + _dot(q_in[:, ks], st.astype(BF16))
        dec_col = jnp.transpose(jnp.broadcast_to(dec_tile[:, ks], (8, GLA_DK)))[:, 0:1]
        st_ref[h] = st * dec_col + _dot_tn(k_tile[:, ks], va[:, vs])
        o = o * lax.rsqrt(jnp.mean(o * o, axis=-1, keepdims=True) + EPS) * gn
        oa_ref[:, vs] = o * _silu(ra[:, vs])

    wd = WINDOW
    k_all = jnp.concatenate([kprev_ref[...], kb], axis=0)
    v_all = jnp.concatenate([vprev_ref[...], vb], axis=0)
    kprev_ref[...] = kb[ts - wd:, :]
    vprev_ref[...] = vb[ts - wd:, :]
    lo_half = lax.broadcasted_iota(jnp.int32, (wd + ts, LANES), 1) < SWA_DH
    k_sw = pltpu.roll(k_all, SWA_DH, 1)
    v_sw = pltpu.roll(v_all, SWA_DH, 1)
    k2 = (jnp.where(lo_half, k_all, k_sw), jnp.where(lo_half, k_sw, k_all))
    v2 = (jnp.where(lo_half, v_all, v_sw), jnp.where(lo_half, v_sw, v_all))
    qi = lax.broadcasted_iota(jnp.int32, (wd, 2 * wd), 0) + wd
    kj = lax.broadcasted_iota(jnp.int32, (wd, 2 * wd), 1)
    band = (qi - kj >= 0) & (qi - kj < wd)
    band_first = band & (jnp.logical_not(first) | (kj >= wd)) if first is not False else band
    q_lo = lax.broadcasted_iota(jnp.int32, (wd, LANES), 1) < SWA_DH
    row0 = lax.broadcasted_iota(jnp.int32, (2 * wd, LANES), 0) == 0
    ones = jnp.ones((2 * wd, LANES), BF16)
    half = SWA_GROUP // 2
    combos = [(blk, hk) for blk in range(ts // wd) for hk in range(SWA_KV_HEADS)]

    def swa_probs(blk, hk):
        parts = []
        for p in range(half):
            qp = qb[blk * wd:(blk + 1) * wd, (hk * half + p) * LANES:(hk * half + p + 1) * LANES]
            parts += [jnp.where(q_lo, qp, jnp.zeros_like(qp)), jnp.where(q_lo, jnp.zeros_like(qp), qp)]
        s = _dot_nt(jnp.concatenate(parts, axis=0), k2[hk][blk * wd:blk * wd + 2 * wd])
        valid = band_first if blk == 0 else band
        es = []
        for r in range(SWA_GROUP):
            head = hk * SWA_GROUP + r
            sh = jnp.where(valid, s[r * wd:(r + 1) * wd], fill_ref[head:head + 1, :])
            es.append(jnp.exp(sh - jnp.max(sh, axis=-1, keepdims=True)).astype(BF16))
        return jnp.concatenate(es, axis=0)

    probs = [swa_probs(blk, hk) for blk, hk in combos]
    side_out = advance()
    ga = _dot(hb, w[GA][...])
    gb = _dot(hb, w[GB][...])
    for (blk, hk), e in zip(combos, probs):
        vv = jnp.where(row0, jnp.zeros_like(ones), v2[hk][blk * wd:blk * wd + 2 * wd])
        o = _dot(e, jnp.concatenate([vv, ones], axis=1))
        for p in range(half):
            oe = o[(2 * p) * wd:(2 * p + 1) * wd]
            oo = o[(2 * p + 1) * wd:(2 * p + 2) * wd]
            num = jnp.where(q_lo, oe[:, :LANES], oo[:, :LANES])
            den = jnp.where(q_lo, oe[:, LANES:], oo[:, LANES:])
            ob_ref[blk * wd:(blk + 1) * wd, (hk * half + p) * LANES:(hk * half + p + 1) * LANES] = num / den

    merged = _sigmoid(ga) * oa_ref[...] + _sigmoid(gb) * ob_ref[...]
    return x + gt_ref[...] * _dot(merged.astype(BF16), w[WO][...]), side_out


def _route_and_sort(hb, wrt_ref, brt_ref, upper_ref):
    tm = hb.shape[0]
    neg = -jnp.inf
    lt = _dot_nt(wrt_ref[...], hb) + brt_ref[...]
    yield None
    lrow = lambda k: lt[ROUTE_STRIDE * k:ROUTE_STRIDE * k + 1, :]
    lg = [lrow(g) for g in range(N_GROUPS)]
    gmax = functools.reduce(jnp.maximum, lg)
    g_w = 1.0 / functools.reduce(jnp.add, [jnp.exp(v - gmax) for v in lg])
    g_idx = _first_of(lg, gmax)
    le = []
    for j in range(EXPERTS_PER_GROUP):
        v = lrow(N_GROUPS + (N_GROUPS - 1) * EXPERTS_PER_GROUP + j)
        for g in range(N_GROUPS - 2, -1, -1):
            v = jnp.where(g_idx == g, lrow(N_GROUPS + g * EXPERTS_PER_GROUP + j), v)
        le.append(v)
    v1 = functools.reduce(jnp.maximum, le)
    i1 = _first_of(le, v1)
    le2 = [jnp.where(i1 == j, neg, le[j]) for j in range(EXPERTS_PER_GROUP)]
    v2 = functools.reduce(jnp.maximum, le2)
    i2 = _first_of(le2, v2)
    e2 = jnp.exp(v2 - v1)
    w1 = g_w / (1.0 + e2)
    w2 = g_w * e2 / (1.0 + e2)

    sub = lax.broadcasted_iota(jnp.int32, (ROW_PACK, tm), 0)
    rank = _dot((sub == g_idx).astype(BF16), upper_ref[...])
    yield None
    pcs, offs, run = [], [], 0
    pos = jnp.zeros((1, tm), F32)
    for g in range(N_GROUPS):
        mine = g_idx == g
        cnt = jnp.sum(mine.astype(F32), axis=1, keepdims=True)[0, 0].astype(jnp.int32)
        pcs.append(((cnt + (ROW_PACK - 1)) // ROW_PACK) * ROW_PACK)
        offs.append(run)
        pos = pos + jnp.where(mine, rank[g:g + 1, :] + jnp.asarray(run, F32), 0.0)
        run = run + pcs[g]
    perm = (lax.broadcasted_iota(jnp.int32, (SORT_ROWS, tm), 0) == pos.astype(jnp.int32)).astype(BF16)
    cw = jnp.where(sub == i1, w1, 0.0) + jnp.where(sub == i2, w2, 0.0)
    cw_hi = cw.astype(BF16).astype(F32)
    packed = cw_hi + pltpu.roll(cw - cw_hi, EXPERTS_PER_GROUP, 0)
    cw_cols = _rows_to_cols(jnp.concatenate([packed, jnp.zeros((LANES - ROW_PACK, tm), F32)], axis=0))
    sorted_rows = _dot(perm, jnp.concatenate([hb, cw_cols.astype(BF16)], axis=1)).astype(BF16)
    yield sorted_rows, pos, pcs, offs


def _mixer_call(x2, vecs1, vecs2, w2p, bgk, gn, fill, wrt, brt, weights, experts_f32, cap, bsz, seq):
    g1, sc1, sh1, gt1 = vecs1
    g2, sc2, sh2 = vecs2
    ns = seq // TM_MIX
    n_steps = bsz * ns
    tok = pl.BlockSpec((TM_MIX, D_MODEL), lambda b, s: (b * ns + s, 0))
    vec = pl.BlockSpec((None, 1, D_MODEL), lambda b, s: (b, 0, 0))
    full = lambda a: pl.BlockSpec(a.shape, lambda b, s: (0,) * a.ndim)
    hbm = pl.BlockSpec(memory_space=pl.ANY)
    slab = lambda a: pl.BlockSpec((a.shape[0] // n_steps, a.shape[1]), lambda b, s: (b * ns + s, 0))
    idx = jnp.arange(MIX_SUB)
    tri = ((idx[:, None] // GLA_CHUNK == idx[None, :] // GLA_CHUNK) & (idx[None, :] <= idx[:, None])).astype(BF16)
    upper = (idx[:, None] < idx[None, :]).astype(BF16)
    ex2d = [e.reshape(-1, e.shape[-1]) for e in experts_f32]
    n_sub = n_steps * SUBS
    vmem_in = [x2, g1, sc1, sh1, gt1, w2p, bgk, tri, gn, fill, g2, sc2, sh2, wrt, brt, upper]
    assert len(vmem_in) + len(ex2d) == N_VMEM_IN + 1
    outs = pl.pallas_call(
        _mixer_kernel,
        grid=(bsz, ns),
        in_specs=[tok, full(g1), vec, vec, vec, full(w2p), full(bgk), full(tri), full(gn), full(fill),
                  full(g2), vec, vec, full(wrt), full(brt), full(upper)]
        + [slab(e) for e in ex2d] + [hbm] * N_WEIGHTS,
        out_specs=[tok, hbm, hbm, pl.BlockSpec(memory_space=pltpu.SMEM)] + [slab(e) for e in ex2d],
        out_shape=[jax.ShapeDtypeStruct(x2.shape, F32),
                   jax.ShapeDtypeStruct((N_GROUPS, cap, HS_W), BF16),
                   jax.ShapeDtypeStruct((n_sub, 8, MIX_SUB), F32),
                   jax.ShapeDtypeStruct((n_sub * META,), jnp.int32)]
        + [jax.ShapeDtypeStruct(e.shape, BF16) for e in ex2d],
        scratch_shapes=[pltpu.VMEM(wt.shape, BF16) for wt in weights]
        + [pltpu.VMEM((GLA_HEADS, GLA_DK, GLA_DV), F32),
           pltpu.VMEM((WINDOW, SWA_KV_W), BF16), pltpu.VMEM((WINDOW, SWA_KV_W), BF16),
           pltpu.VMEM((MIX_SUB, GLA_VAL_W), F32), pltpu.VMEM((MIX_SUB, SWA_W), F32),
           pltpu.VMEM((SUBS, SORT_ROWS, HS_W), BF16), pltpu.VMEM((SUBS, 8, MIX_SUB), F32),
           pltpu.VMEM((MIX_SUB, D_MODEL), BF16), pltpu.VMEM((TM_EXP, HS_W), BF16),
           pltpu.SMEM((N_GROUPS,), jnp.int32), pltpu.SemaphoreType.DMA((SUBS,))],
        compiler_params=_params(("arbitrary", "arbitrary")),
        name="mixer",
    )(*vmem_in, *ex2d, *weights)
    x1, hs, pos, meta = outs[:4]
    experts_bf16 = [o.reshape(e.shape) for o, e in zip(outs[4:], experts_f32)]
    return x1, hs, pos, meta, experts_bf16


def _experts_kernel(grp_ref, blk_ref, valid_ref, hs_ref, wg_ref, wu_ref, wd_ref, ys_ref):
    @pl.when(valid_ref[pl.program_id(0)] != 0)
    def _():
        hx = hs_ref[...]
        h = hx[:, :D_MODEL]
        cw = hx[:, D_MODEL:].astype(F32)
        gates = [_dot(h, wg_ref[j]) for j in range(EXPERTS_PER_GROUP)]
        ups = [_dot(h, wu_ref[j]) for j in range(EXPERTS_PER_GROUP)]
        hids = [(_silu(gates[j]) * ups[j]
                 * (cw[:, j:j + 1] + cw[:, EXPERTS_PER_GROUP + j:EXPERTS_PER_GROUP + j + 1])).astype(BF16)
                for j in range(EXPERTS_PER_GROUP)]
        y = _dot(hids[0], wd_ref[0])
        for j in range(1, EXPERTS_PER_GROUP):
            y = y + _dot(hids[j], wd_ref[j])
        ys_ref[...] = y.astype(BF16)


def _experts_call(grp, blk, valid, hs, wg, wu, wd):
    n_work = grp.shape[0]
    rows = lambda width: pl.BlockSpec((None, TM_EXP, width), lambda i, grp, blk, valid: (grp[i], blk[i], 0))
    wspec = lambda a: pl.BlockSpec((EXPERTS_PER_GROUP,) + a.shape[1:], lambda i, grp, blk, valid: (grp[i], 0, 0))
    return pl.pallas_call(
        _experts_kernel,
        grid_spec=pltpu.PrefetchScalarGridSpec(
            num_scalar_prefetch=3,
            grid=(n_work,),
            in_specs=[rows(HS_W), wspec(wg), wspec(wu), wspec(wd)],
            out_specs=rows(D_MODEL)),
        out_shape=jax.ShapeDtypeStruct(hs.shape[:2] + (D_MODEL,), BF16),
        compiler_params=_params(("arbitrary",)),
        name="experts",
    )(grp, blk, valid, hs, wg, wu, wd)


def _final_kernel(meta_ref, x1_ref, pos_ref, gt_ref, gf_ref, ys_hbm, o_ref, ybuf_ref, sem):
    step = pl.program_id(0)

    def fetch(s, action):
        half = s % 2
        for sub in range(FIN_SUBS):
            m0 = (s * FIN_SUBS + sub) * META
            bases = [meta_ref[m0 + g] for g in range(N_GROUPS)]
            pcs = [meta_ref[m0 + N_GROUPS + g] for g in range(N_GROUPS)]
            offs, run = [], 0
            for g in range(N_GROUPS):
                offs.append(run)
                run = run + pcs[g]
            _segment_copies(pcs, bases, offs, lambda g, src_row, dst_row, size, sub=sub: pltpu.make_async_copy(
                ys_hbm.at[g, pl.ds(src_row, size), :], ybuf_ref.at[half, sub, pl.ds(dst_row, size), :],
                sem.at[half]), action)

    @pl.when(step == 0)
    def _():
        ybuf_ref[...] = jnp.zeros_like(ybuf_ref)
        fetch(step, "start")

    @pl.when(step + 1 < pl.num_programs(0))
    def _():
        fetch(step + 1, "start")

    fetch(step, "wait")
    half = step % 2
    for sub in range(FIN_SUBS):
        r0 = sub * MIX_SUB
        pos_col = _rows_to_cols(pos_ref[sub])[:, 0:1].astype(jnp.int32)
        sel = (lax.broadcasted_iota(jnp.int32, (MIX_SUB, SORT_ROWS), 1) == pos_col).astype(BF16)
        y = _dot(sel, ybuf_ref[half, sub])
        o_ref[r0:r0 + MIX_SUB, :] = _rms(x1_ref[r0:r0 + MIX_SUB, :] + gt_ref[...] * y, gf_ref[...])


def _final_call(meta, x1, pos, gt2, gf, ys, seq):
    t = x1.shape[0]
    per_b = seq // TM_FIN
    tok = pl.BlockSpec((TM_FIN, D_MODEL), lambda i, meta: (i, 0))
    return pl.pallas_call(
        _final_kernel,
        grid_spec=pltpu.PrefetchScalarGridSpec(
            num_scalar_prefetch=1,
            grid=(t // TM_FIN,),
            in_specs=[tok, pl.BlockSpec((FIN_SUBS, 8, MIX_SUB), lambda i, meta: (i, 0, 0)),
                      pl.BlockSpec((None, 1, D_MODEL), lambda i, meta: (i // per_b, 0, 0)),
                      pl.BlockSpec((1, D_MODEL), lambda i, meta: (0, 0)),
                      pl.BlockSpec(memory_space=pl.ANY)],
            out_specs=tok,
            scratch_shapes=[pltpu.VMEM((2, FIN_SUBS, SORT_ROWS, D_MODEL), BF16), pltpu.SemaphoreType.DMA((2,))]),
        out_shape=jax.ShapeDtypeStruct((t, D_MODEL), F32),
        compiler_params=_params(("arbitrary",)),
        name="final",
    )(meta, x1, pos, gt2, gf, ys)


def kernel(x, c, w_ada, b_ada, norm1_g, w_in, w_gk2, b_gk, gla_norm_g, sink, w_o, norm2_g,
           w_group, b_group, w_router, b_router, w_gate, w_up, w_down, norm_f_g):
    bsz, seq, d = x.shape
    assert w_ada.shape[0] == 1, "single layer: the final norm is fused behind the layer's MoE"
    l = 0
    t = bsz * seq
    n_sub = t // MIX_SUB
    cap = -(-(t + n_sub * (ROW_PACK - 1) + TM_EXP) // TM_EXP) * TM_EXP
    n_work = t // TM_EXP + n_sub * (ROW_PACK - 1) * N_GROUPS // TM_EXP + N_GROUPS + 1

    mod = _mod_call(c, w_ada[l], b_ada[l])
    sh1, sc1, gt1, sh2, sc2, gt2 = [m.reshape(bsz, 1, d) for m in jnp.split(mod, N_MOD, axis=-1)]

    w_parts = _split_cols(w_in[l])
    w_parts[GATE] = jnp.pad(w_parts[GATE], ((0, 0), (0, LANES - GLA_GATE_RANK)))
    weights = [wt.astype(BF16) for wt in w_parts + [w_o[l]]]
    w2p = jnp.pad(w_gk2[l], ((0, LANES - GLA_GATE_RANK), (0, 0))).astype(BF16)
    fill = jnp.full((SWA_HEADS, 2 * WINDOW), -jnp.inf, F32).at[:, 0].set(sink[l])
    n_logit = N_GROUPS + N_EXPERTS
    wrt = jnp.zeros((n_logit, ROUTE_STRIDE, d), F32).at[:, 0, :].set(
        jnp.concatenate([w_group[l], w_router[l]], axis=1).T).reshape(n_logit * ROUTE_STRIDE, d).astype(BF16)
    brt = jnp.zeros((n_logit, ROUTE_STRIDE), F32).at[:, 0].set(
        jnp.concatenate([b_group[l], b_router[l]])).reshape(n_logit * ROUTE_STRIDE, 1)

    x1, hs, pos, meta, (wg, wu, wd) = _mixer_call(
        x.reshape(t, d), (norm1_g[l].reshape(1, d), sc1, sh1, gt1), (norm2_g[l].reshape(1, d), sc2, sh2),
        w2p, b_gk[l].reshape(1, GLA_KEY_W), gla_norm_g[l].reshape(1, GLA_DV), fill, wrt, brt, weights,
        (w_gate[l], w_up[l], w_down[l]), cap, bsz, seq)

    last = meta.reshape(n_sub, META)[-1]
    totals = last[:N_GROUPS] + last[N_GROUPS:]
    ends = jnp.cumsum((totals + TM_EXP - 1) // TM_EXP)
    item = jnp.minimum(jnp.arange(n_work, dtype=jnp.int32), ends[-1] - 1)
    grp = jnp.sum(item[:, None] >= ends[None, :], axis=1).astype(jnp.int32)
    blk = item - jnp.concatenate([jnp.zeros((1,), ends.dtype), ends[:-1]])[grp]
    valid = (jnp.arange(n_work) < ends[-1]).astype(jnp.int32)

    ys = _experts_call(grp, blk.astype(jnp.int32), valid, hs, wg, wu, wd)
    out = _final_call(meta, x1, pos, gt2, norm_f_g.reshape(1, d), ys, seq)
    return out.reshape(bsz, seq, d)


def _split_cols(w):
    parts, start = [], 0
    for width in IN_WIDTHS:
        parts.append(w[:, start:start + width])
        start += width
    return parts
```

```python
import functools

import jax
import jax.numpy as jnp
from jax import lax
from jax.experimental import pallas as pl
from jax.experimental.pallas import tpu as pltpu

F32 = jnp.float32
BF16 = jnp.bfloat16

D_MODEL = 1024
GLA_HEADS = 4
GLA_DK = 128
GLA_DV = 256
GLA_KEY_W = GLA_HEADS * GLA_DK
GLA_VAL_W = GLA_HEADS * GLA_DV
GLA_GATE_RANK = 16
GLA_TAU = 16.0
GLA_CHUNK = 64
SWA_HEADS = 16
SWA_KV_HEADS = 2
SWA_GROUP = SWA_HEADS // SWA_KV_HEADS
SWA_DH = 64
SWA_W = SWA_HEADS * SWA_DH
SWA_KV_W = SWA_KV_HEADS * SWA_DH
WINDOW = 128
N_GROUPS = 4
EXPERTS_PER_GROUP = 4
N_EXPERTS = N_GROUPS * EXPERTS_PER_GROUP
D_EXPERT = 256
N_MOD = 6
EPS = 1e-6
IN_WIDTHS = (GLA_KEY_W, GLA_KEY_W, GLA_VAL_W, GLA_VAL_W, GLA_GATE_RANK,
             SWA_W, SWA_KV_W, SWA_KV_W, D_MODEL, D_MODEL)

LANES = 128
LOG2_E = 1.4426950408889634
LN_2 = 0.6931471805599453
VMEM_LIMIT = 56 * 1024 * 1024

TM_MIX = 512
MIX_SUB = 256
SUBS = TM_MIX // MIX_SUB
TM_FIN = 1024
FIN_SUBS = TM_FIN // MIX_SUB
ROW_PACK = 16
SORT_ROWS = MIX_SUB + 64
SEG_BITS = (256, 128, 64, 32, 16)
HS_W = D_MODEL + LANES
TM_EXP = 512
META = 2 * N_GROUPS
ROUTE_STRIDE = 8
TN_MOD = 384


def _params(sem):
    return pltpu.CompilerParams(dimension_semantics=sem, vmem_limit_bytes=VMEM_LIMIT)


def _split(a):
    hi = a.astype(BF16)
    lo = (a - hi.astype(F32)).astype(BF16)
    return hi, lo


def _dot(a, b):
    return jnp.dot(a, b, preferred_element_type=F32)


def _dot_nt(a, b):
    return lax.dot_general(a, b, (((1,), (1,)), ((), ())), preferred_element_type=F32)


def _dot_tn(a, b):
    return lax.dot_general(a, b, (((0,), (0,)), ((), ())), preferred_element_type=F32)


def _dot3(a, b):
    a_hi, a_lo = _split(a)
    b_hi, b_lo = _split(b)
    return _dot(a_hi, b_hi) + _dot(a_hi, b_lo) + _dot(a_lo, b_hi)


def _sigmoid(x):
    return 1.0 / (1.0 + jnp.exp(-x))


def _silu(x):
    return x * _sigmoid(x)


def _rms(x, g):
    return x * lax.rsqrt(jnp.mean(x * x, axis=-1, keepdims=True) + EPS) * g


def _first_of(vals, target):
    idx = jnp.full(target.shape, len(vals) - 1, jnp.int32)
    for j in range(len(vals) - 2, -1, -1):
        idx = jnp.where(vals[j] == target, j, idx)
    return idx


def _rows_to_cols(a):
    return jnp.concatenate([jnp.transpose(a[:, i * LANES:(i + 1) * LANES]) for i in range(a.shape[1] // LANES)],
                           axis=0)


def _segment_copies(pcs, src_rows, dst_rows, make_copy, action):
    for g in range(N_GROUPS):
        for bit in SEG_BITS:
            done = pcs[g] & ~(2 * bit - 1)

            @pl.when((pcs[g] & bit) != 0)
            def _(g=g, bit=bit, done=done):
                cp = make_copy(g, pl.multiple_of(src_rows[g] + done, ROW_PACK),
                               pl.multiple_of(dst_rows[g] + done, ROW_PACK), bit)
                cp.start() if action == "start" else cp.wait()


QA, KA, VA, RA, GATE, QB, KB, VB, GA, GB, WO = range(11)


def _mod_kernel(c_ref, w_ref, b_ref, win_ref, wo_ref, o_ref, *part_refs):
    o_ref[...] = _dot3(_silu(c_ref[...]), w_ref[...]) + b_ref[...]
    slab = win_ref[...]
    start = 0
    for slot, width in enumerate(IN_WIDTHS):
        if slot == GATE:
            lane = lax.broadcasted_iota(jnp.int32, (slab.shape[0], LANES), 1)
            part = jnp.where(lane < width, slab[:, start:start + LANES], 0.0)
        else:
            part = slab[:, start:start + width]
        part_refs[slot][...] = part.astype(BF16)
        start += width
    part_refs[WO][...] = wo_ref[...].astype(BF16)


def _mod_call(c, w_ada, b_ada, w_in, w_o):
    bsz = c.shape[0]
    n = w_ada.shape[1]
    steps = n // TN_MOD
    rows = D_MODEL // steps
    widths = [LANES if slot == GATE else width for slot, width in enumerate(IN_WIDTHS)] + [D_MODEL]
    slab = lambda width: pl.BlockSpec((rows, width), lambda j: (j, 0))
    outs = pl.pallas_call(
        _mod_kernel,
        grid=(steps,),
        in_specs=[pl.BlockSpec((bsz, D_MODEL), lambda j: (0, 0)),
                  pl.BlockSpec((D_MODEL, TN_MOD), lambda j: (0, j)),
                  pl.BlockSpec((1, TN_MOD), lambda j: (0, j)),
                  slab(w_in.shape[1]), slab(D_MODEL)],
        out_specs=[pl.BlockSpec((bsz, TN_MOD), lambda j: (0, j))] + [slab(width) for width in widths],
        out_shape=[jax.ShapeDtypeStruct((bsz, n), F32)]
        + [jax.ShapeDtypeStruct((D_MODEL, width), BF16) for width in widths],
        compiler_params=_params(("parallel",)),
        name="mod",
    )(c, w_ada, b_ada.reshape(1, n), w_in, w_o)
    return outs[0], list(outs[1:])


N_WEIGHTS = 11
N_VMEM_IN = 18


def _mixer_kernel(x_ref, g1_ref, sc1_ref, sh1_ref, gt1_ref, w2_ref, bgk_ref, tri_ref, gn_ref, fill_ref,
                  g2_ref, sc2_ref, sh2_ref, wrt_ref, brt_ref, upper_ref, wgf_ref, wuf_ref, wdf_ref, *rest):
    w_hbm = rest[:N_WEIGHTS]
    x1_ref, hs_hbm, pos_hbm, meta_ref, wgb_ref, wub_ref, wdb_ref = rest[N_WEIGHTS:N_WEIGHTS + 7]
    w = rest[N_WEIGHTS + 7:2 * N_WEIGHTS + 7]
    (st_ref, kprev_ref, vprev_ref, oa_ref, ob_ref, stage_ref, pstage_ref, hkeep_ref, zero_ref, base_ref,
     sem) = rest[2 * N_WEIGHTS + 7:]
    first = pl.program_id(1) == 0
    step = pl.program_id(0) * pl.num_programs(1) + pl.program_id(1)
    n_steps = pl.num_programs(0) * pl.num_programs(1)

    @pl.when(step == 0)
    def _():
        for src, dst in zip(w_hbm, w):
            pltpu.sync_copy(src, dst)
        for g in range(N_GROUPS):
            base_ref[g] = 0
        hkeep_ref[...] = jnp.zeros_like(hkeep_ref)

    @pl.when(first)
    def _():
        st_ref[...] = jnp.zeros_like(st_ref)
        kprev_ref[...] = jnp.zeros_like(kprev_ref)
        vprev_ref[...] = jnp.zeros_like(vprev_ref)

    wgb_ref[...] = wgf_ref[...].astype(BF16)
    wub_ref[...] = wuf_ref[...].astype(BF16)
    wdb_ref[...] = wdf_ref[...].astype(BF16)

    def stage_copy(slot):
        return lambda g, src_row, dst_row, size: pltpu.make_async_copy(
            stage_ref.at[slot, pl.ds(src_row, size), :], hs_hbm.at[g, pl.ds(dst_row, size), :], sem.at[slot])

    pos_copy = lambda slot, tile: pltpu.make_async_copy(pstage_ref.at[slot], pos_hbm.at[tile], sem.at[slot])
    zeros4 = [0] * N_GROUPS

    def publish(sub, owner, routed, drain):
        sorted_rows, pos, pcs, offs = routed
        tile = owner * SUBS + sub
        m0 = tile * META

        @pl.when(owner > 0)
        def _():
            prev = [meta_ref[m0 - SUBS * META + N_GROUPS + g] for g in range(N_GROUPS)]
            _segment_copies(prev, zeros4, zeros4, stage_copy(sub), "wait")
            pos_copy(sub, tile).wait()

        stage_ref[sub] = sorted_rows
        pstage_ref[sub] = jnp.broadcast_to(pos, (8, MIX_SUB))
        bases = [base_ref[g] for g in range(N_GROUPS)]
        for g in range(N_GROUPS):
            meta_ref[m0 + g] = bases[g]
            meta_ref[m0 + N_GROUPS + g] = pcs[g]
            base_ref[g] = bases[g] + pcs[g]
        _segment_copies(pcs, offs, bases, stage_copy(sub), "start")
        pos_copy(sub, tile).start()

        def wait_own():
            _segment_copies(pcs, zeros4, zeros4, stage_copy(sub), "wait")
            pos_copy(sub, tile).wait()

        if drain is True:
            wait_own()
        elif drain is not False:
            pl.when(drain)(wait_own)

    route = lambda hb: _route_and_sort(hb, wrt_ref, brt_ref, upper_ref)
    moe_in = lambda x1: (_rms(x1, g2_ref[...]) * (1.0 + sc2_ref[...]) + sh2_ref[...]).astype(BF16)
    last_step = step == n_steps - 1

    routing, results = route(hkeep_ref[...]), []
    for sub in range(SUBS):
        r0 = sub * MIX_SUB
        x1, routed = _mixer_tile(x_ref[r0:r0 + MIX_SUB, :], first if sub == 0 else False, g1_ref, sc1_ref,
                                 sh1_ref, gt1_ref, w2_ref, bgk_ref, tri_ref, gn_ref, fill_ref, w, st_ref,
                                 kprev_ref, vprev_ref, oa_ref, ob_ref, side=routing)
        x1_ref[r0:r0 + MIX_SUB, :] = x1
        results.append(routed)
        if sub + 1 < SUBS:
            routing = route(moe_in(x1))
        else:
            hkeep_ref[...] = moe_in(x1)

    @pl.when(step > 0)
    def _():
        publish(SUBS - 1, step - 1, results[0], False)

    for sub in range(1, SUBS):
        publish(sub - 1, step, results[sub], last_step)

    @pl.when(last_step)
    def _():
        routing = route(hkeep_ref[...])
        next(routing)
        next(routing)
        publish(SUBS - 1, step, next(routing), True)
        zero_ref[...] = jnp.zeros_like(zero_ref)
        tails = [pltpu.make_async_copy(zero_ref, hs_hbm.at[g, pl.ds(pl.multiple_of(base_ref[g], ROW_PACK), TM_EXP), :],
                                       sem.at[0]) for g in range(N_GROUPS)]
        for cp in tails:
            cp.start()
        for cp in tails:
            cp.wait()


def _mixer_tile(x, first, g1_ref, sc_ref, sh_ref, gt_ref, w2_ref, bgk_ref, tri_ref, gn_ref, fill_ref,
                w, st_ref, kprev_ref, vprev_ref, oa_ref, ob_ref, side=None):
    ts = MIX_SUB
    advance = (lambda: next(side)) if side is not None else (lambda: None)
    hb = (_rms(x, g1_ref[...]) * (1.0 + sc_ref[...]) + sh_ref[...]).astype(BF16)

    z = _dot(hb, w[GATE][...])
    gz = _dot(z.astype(BF16), w2_ref[...]) + bgk_ref[...]
    soft = jnp.log2(1.0 + jnp.exp2(jnp.abs(gz) * (-LOG2_E)))
    la_hi, la_lo = _split(jnp.minimum(gz, 0.0) * (1.0 / GLA_TAU) - soft * (LN_2 / GLA_TAU))
    qa = _dot(hb, w[QA][...])
    ka = _dot(hb, w[KA][...])
    advance()
    va = _dot(hb, w[VA][...]).astype(BF16)
    bc = _dot(tri_ref[...], la_hi) + _dot(tri_ref[...], la_lo)
    ra = _dot(hb, w[RA][...])
    advance()
    qb = _dot(hb, w[QB][...]).astype(BF16) * jnp.asarray(SWA_DH ** -0.5, BF16)
    kb = _dot(hb, w[KB][...]).astype(BF16)
    vb = _dot(hb, w[VB][...]).astype(BF16)

    nc = ts // GLA_CHUNK
    chunk = lambda c: slice(c * GLA_CHUNK, (c + 1) * GLA_CHUNK)
    btot = [bc[(c + 1) * GLA_CHUNK - 1:(c + 1) * GLA_CHUNK, :] for c in range(nc)]
    pre = [jnp.zeros_like(btot[0])]
    for c in range(nc):
        pre.append(pre[c] + btot[c])
    rows = lambda vecs: jnp.concatenate([jnp.broadcast_to(v, (GLA_CHUNK, GLA_KEY_W)) for v in vecs], axis=0)
    qf = qa * (GLA_DK ** -0.5) * jnp.exp(bc)
    q_dec = qf.astype(BF16)
    q_in = (qf * rows([jnp.exp(pre[c]) for c in range(nc)])).astype(BF16)
    k_inv = (ka * jnp.exp(-bc)).astype(BF16)
    k_end = ka * jnp.exp(rows(btot) - bc)
    k_tile = (k_end * rows([jnp.exp(pre[nc] - pre[c + 1]) for c in range(nc)])).astype(BF16)
    k_cross = {(c, c2): (k_end[chunk(c2)] * jnp.exp(pre[c] - pre[c2 + 1])).astype(BF16)
               for c in range(nc) for c2 in range(c)}
    dec_tile = jnp.exp(pre[nc])
    causal = (lax.broadcasted_iota(jnp.int32, (ts, ts), 1) <= lax.broadcasted_iota(jnp.int32, (ts, ts), 0))

    def gla_scores(h):
        ks = slice(h * GLA_DK, (h + 1) * GLA_DK)
        att = []
        for c in range(nc):
            keys = [k_cross[(c, c2)][:, ks] for c2 in range(c)] + [k_inv[chunk(c), ks]]
            keys += [jnp.zeros((GLA_CHUNK, GLA_DK), BF16)] * (nc - 1 - c)
            att.append(_dot_nt(q_dec[chunk(c), ks], jnp.concatenate(keys, axis=0)))
        return jnp.where(causal, jnp.concatenate(att, axis=0), 0.0).astype(BF16)

    atts = [gla_scores(h) for h in range(GLA_HEADS)]
    gn = gn_ref[...]
    for h in range(GLA_HEADS):
        ks = slice(h * GLA_DK, (h + 1) * GLA_DK)
        vs = slice(h * GLA_DV, (h + 1) * GLA_DV)
        st = st_ref[h]
        o = _dot(atts[h], va[:, vs]) + _dot(q_in[:, ks], st.astype(BF16))
        dec_col = jnp.transpose(jnp.broadcast_to(dec_tile[:, ks], (8, GLA_DK)))[:, 0:1]
        st_ref[h] = st * dec_col + _dot_tn(k_tile[:, ks], va[:, vs])
        o = o * lax.rsqrt(jnp.mean(o * o, axis=-1, keepdims=True) + EPS) * gn
        oa_ref[:, vs] = o * _silu(ra[:, vs])

    wd = WINDOW
    k_all = jnp.concatenate([kprev_ref[...], kb], axis=0)
    v_all = jnp.concatenate([vprev_ref[...], vb], axis=0)
    kprev_ref[...] = kb[ts - wd:, :]
    vprev_ref[...] = vb[ts - wd:, :]
    lo_half = lax.broadcasted_iota(jnp.int32, (wd + ts, LANES), 1) < SWA_DH
    k_sw = pltpu.roll(k_all, SWA_DH, 1)
    v_sw = pltpu.roll(v_all, SWA_DH, 1)
    k2 = (jnp.where(lo_half, k_all, k_sw), jnp.where(lo_half, k_sw, k_all))
    v2 = (jnp.where(lo_half, v_all, v_sw), jnp.where(lo_half, v_sw, v_all))
    qi = lax.broadcasted_iota(jnp.int32, (wd, 2 * wd), 0) + wd
    kj = lax.broadcasted_iota(jnp.int32, (wd, 2 * wd), 1)
    band = (qi - kj >= 0) & (qi - kj < wd)
    band_first = band & (jnp.logical_not(first) | (kj >= wd)) if first is not False else band
    q_lo = lax.broadcasted_iota(jnp.int32, (wd, LANES), 1) < SWA_DH
    row0 = lax.broadcasted_iota(jnp.int32, (2 * wd, LANES), 0) == 0
    ones = jnp.ones((2 * wd, LANES), BF16)
    half = SWA_GROUP // 2
    combos = [(blk, hk) for blk in range(ts // wd) for hk in range(SWA_KV_HEADS)]

    def swa_probs(blk, hk):
        parts = []
        for p in range(half):
            qp = qb[blk * wd:(blk + 1) * wd, (hk * half + p) * LANES:(hk * half + p + 1) * LANES]
            parts += [jnp.where(q_lo, qp, jnp.zeros_like(qp)), jnp.where(q_lo, jnp.zeros_like(qp), qp)]
        s = _dot_nt(jnp.concatenate(parts, axis=0), k2[hk][blk * wd:blk * wd + 2 * wd])
        valid = band_first if blk == 0 else band
        es = []
        for r in range(SWA_GROUP):
            head = hk * SWA_GROUP + r
            sh = jnp.where(valid, s[r * wd:(r + 1) * wd], fill_ref[head:head + 1, :])
            es.append(jnp.exp(sh - jnp.max(sh, axis=-1, keepdims=True)).astype(BF16))
        return jnp.concatenate(es, axis=0)

    probs = [swa_probs(blk, hk) for blk, hk in combos]
    side_out = advance()
    ga = _dot(hb, w[GA][...])
    gb = _dot(hb, w[GB][...])
    for (blk, hk), e in zip(combos, probs):
        vv = jnp.where(row0, jnp.zeros_like(ones), v2[hk][blk * wd:blk * wd + 2 * wd])
        o = _dot(e, jnp.concatenate([vv, ones], axis=1))
        for p in range(half):
            oe = o[(2 * p) * wd:(2 * p + 1) * wd]
            oo = o[(2 * p + 1) * wd:(2 * p + 2) * wd]
            num = jnp.where(q_lo, oe[:, :LANES], oo[:, :LANES])
            den = jnp.where(q_lo, oe[:, LANES:], oo[:, LANES:])
            ob_ref[blk * wd:(blk + 1) * wd, (hk * half + p) * LANES:(hk * half + p + 1) * LANES] = num / den

    merged = _sigmoid(ga) * oa_ref[...] + _sigmoid(gb) * ob_ref[...]
    return x + gt_ref[...] * _dot(merged.astype(BF16), w[WO][...]), side_out


def _route_and_sort(hb, wrt_ref, brt_ref, upper_ref):
    tm = hb.shape[0]
    neg = -jnp.inf
    lt = _dot_nt(wrt_ref[...], hb) + brt_ref[...]
    yield None
    lrow = lambda k: lt[ROUTE_STRIDE * k:ROUTE_STRIDE * k + 1, :]
    lg = [lrow(g) for g in range(N_GROUPS)]
    gmax = functools.reduce(jnp.maximum, lg)
    g_w = 1.0 / functools.reduce(jnp.add, [jnp.exp(v - gmax) for v in lg])
    g_idx = _first_of(lg, gmax)
    le = []
    for j in range(EXPERTS_PER_GROUP):
        v = lrow(N_GROUPS + (N_GROUPS - 1) * EXPERTS_PER_GROUP + j)
        for g in range(N_GROUPS - 2, -1, -1):
            v = jnp.where(g_idx == g, lrow(N_GROUPS + g * EXPERTS_PER_GROUP + j), v)
        le.append(v)
    v1 = functools.reduce(jnp.maximum, le)
    i1 = _first_of(le, v1)
    le2 = [jnp.where(i1 == j, neg, le[j]) for j in range(EXPERTS_PER_GROUP)]
    v2 = functools.reduce(jnp.maximum, le2)
    i2 = _first_of(le2, v2)
    e2 = jnp.exp(v2 - v1)
    w1 = g_w / (1.0 + e2)
    w2 = g_w * e2 / (1.0 + e2)

    sub = lax.broadcasted_iota(jnp.int32, (ROW_PACK, tm), 0)
    rank = _dot((sub == g_idx).astype(BF16), upper_ref[...])
    yield None
    pcs, offs, run = [], [], 0
    pos = jnp.zeros((1, tm), F32)
    for g in range(N_GROUPS):
        mine = g_idx == g
        cnt = jnp.sum(mine.astype(F32), axis=1, keepdims=True)[0, 0].astype(jnp.int32)
        pcs.append(((cnt + (ROW_PACK - 1)) // ROW_PACK) * ROW_PACK)
        offs.append(run)
        pos = pos + jnp.where(mine, rank[g:g + 1, :] + jnp.asarray(run, F32), 0.0)
        run = run + pcs[g]
    perm = (lax.broadcasted_iota(jnp.int32, (SORT_ROWS, tm), 0) == pos.astype(jnp.int32)).astype(BF16)
    cw = jnp.where(sub == i1, w1, 0.0) + jnp.where(sub == i2, w2, 0.0)
    cw_hi = cw.astype(BF16).astype(F32)
    packed = cw_hi + pltpu.roll(cw - cw_hi, EXPERTS_PER_GROUP, 0)
    cw_cols = _rows_to_cols(jnp.concatenate([packed, jnp.zeros((LANES - ROW_PACK, tm), F32)], axis=0))
    sorted_rows = _dot(perm, jnp.concatenate([hb, cw_cols.astype(BF16)], axis=1)).astype(BF16)
    yield sorted_rows, pos, pcs, offs


def _mixer_call(x2, vecs1, vecs2, w2p, bgk, gn, fill, wrt, brt, weights, experts_f32, cap, bsz, seq):
    g1, sc1, sh1, gt1 = vecs1
    g2, sc2, sh2 = vecs2
    ns = seq // TM_MIX
    n_steps = bsz * ns
    tok = pl.BlockSpec((TM_MIX, D_MODEL), lambda b, s: (b * ns + s, 0))
    vec = pl.BlockSpec((None, 1, D_MODEL), lambda b, s: (b, 0, 0))
    full = lambda a: pl.BlockSpec(a.shape, lambda b, s: (0,) * a.ndim)
    hbm = pl.BlockSpec(memory_space=pl.ANY)
    slab = lambda a: pl.BlockSpec((a.shape[0] // n_steps, a.shape[1]), lambda b, s: (b * ns + s, 0))
    idx = jnp.arange(MIX_SUB)
    tri = ((idx[:, None] // GLA_CHUNK == idx[None, :] // GLA_CHUNK) & (idx[None, :] <= idx[:, None])).astype(BF16)
    upper = (idx[:, None] < idx[None, :]).astype(BF16)
    ex2d = [e.reshape(-1, e.shape[-1]) for e in experts_f32]
    n_sub = n_steps * SUBS
    vmem_in = [x2, g1, sc1, sh1, gt1, w2p, bgk, tri, gn, fill, g2, sc2, sh2, wrt, brt, upper]
    assert len(vmem_in) + len(ex2d) == N_VMEM_IN + 1
    outs = pl.pallas_call(
        _mixer_kernel,
        grid=(bsz, ns),
        in_specs=[tok, full(g1), vec, vec, vec, full(w2p), full(bgk), full(tri), full(gn), full(fill),
                  full(g2), vec, vec, full(wrt), full(brt), full(upper)]
        + [slab(e) for e in ex2d] + [hbm] * N_WEIGHTS,
        out_specs=[tok, hbm, hbm, pl.BlockSpec(memory_space=pltpu.SMEM)] + [slab(e) for e in ex2d],
        out_shape=[jax.ShapeDtypeStruct(x2.shape, F32),
                   jax.ShapeDtypeStruct((N_GROUPS, cap, HS_W), BF16),
                   jax.ShapeDtypeStruct((n_sub, 8, MIX_SUB), F32),
                   jax.ShapeDtypeStruct((n_sub * META,), jnp.int32)]
        + [jax.ShapeDtypeStruct(e.shape, BF16) for e in ex2d],
        scratch_shapes=[pltpu.VMEM(wt.shape, BF16) for wt in weights]
        + [pltpu.VMEM((GLA_HEADS, GLA_DK, GLA_DV), F32),
           pltpu.VMEM((WINDOW, SWA_KV_W), BF16), pltpu.VMEM((WINDOW, SWA_KV_W), BF16),
           pltpu.VMEM((MIX_SUB, GLA_VAL_W), F32), pltpu.VMEM((MIX_SUB, SWA_W), F32),
           pltpu.VMEM((SUBS, SORT_ROWS, HS_W), BF16), pltpu.VMEM((SUBS, 8, MIX_SUB), F32),
           pltpu.VMEM((MIX_SUB, D_MODEL), BF16), pltpu.VMEM((TM_EXP, HS_W), BF16),
           pltpu.SMEM((N_GROUPS,), jnp.int32), pltpu.SemaphoreType.DMA((SUBS,))],
        compiler_params=_params(("arbitrary", "arbitrary")),
        name="mixer",
    )(*vmem_in, *ex2d, *weights)
    x1, hs, pos, meta = outs[:4]
    experts_bf16 = [o.reshape(e.shape) for o, e in zip(outs[4:], experts_f32)]
    return x1, hs, pos, meta, experts_bf16


def _experts_kernel(grp_ref, blk_ref, valid_ref, hs_ref, wg_ref, wu_ref, wd_ref, ys_ref):
    @pl.when(valid_ref[pl.program_id(0)] != 0)
    def _():
        hx = hs_ref[...]
        h = hx[:, :D_MODEL]
        cw = hx[:, D_MODEL:].astype(F32)
        gates = [_dot(h, wg_ref[j]) for j in range(EXPERTS_PER_GROUP)]
        ups = [_dot(h, wu_ref[j]) for j in range(EXPERTS_PER_GROUP)]
        hids = [(_silu(gates[j]) * ups[j]
                 * (cw[:, j:j + 1] + cw[:, EXPERTS_PER_GROUP + j:EXPERTS_PER_GROUP + j + 1])).astype(BF16)
                for j in range(EXPERTS_PER_GROUP)]
        y = _dot(hids[0], wd_ref[0])
        for j in range(1, EXPERTS_PER_GROUP):
            y = y + _dot(hids[j], wd_ref[j])
        ys_ref[...] = y.astype(BF16)


def _experts_call(grp, blk, valid, hs, wg, wu, wd):
    n_work = grp.shape[0]
    rows = lambda width: pl.BlockSpec((None, TM_EXP, width), lambda i, grp, blk, valid: (grp[i], blk[i], 0))
    wspec = lambda a: pl.BlockSpec((EXPERTS_PER_GROUP,) + a.shape[1:], lambda i, grp, blk, valid: (grp[i], 0, 0))
    return pl.pallas_call(
        _experts_kernel,
        grid_spec=pltpu.PrefetchScalarGridSpec(
            num_scalar_prefetch=3,
            grid=(n_work,),
            in_specs=[rows(HS_W), wspec(wg), wspec(wu), wspec(wd)],
            out_specs=rows(D_MODEL)),
        out_shape=jax.ShapeDtypeStruct(hs.shape[:2] + (D_MODEL,), BF16),
        compiler_params=_params(("arbitrary",)),
        name="experts",
    )(grp, blk, valid, hs, wg, wu, wd)


def _final_kernel(meta_ref, x1_ref, pos_ref, gt_ref, gf_ref, ys_hbm, o_ref, ybuf_ref, sem):
    step = pl.program_id(0)

    def fetch(s, action):
        half = s % 2
        for sub in range(FIN_SUBS):
            m0 = (s * FIN_SUBS + sub) * META
            bases = [meta_ref[m0 + g] for g in range(N_GROUPS)]
            pcs = [meta_ref[m0 + N_GROUPS + g] for g in range(N_GROUPS)]
            offs, run = [], 0
            for g in range(N_GROUPS):
                offs.append(run)
                run = run + pcs[g]
            _segment_copies(pcs, bases, offs, lambda g, src_row, dst_row, size, sub=sub: pltpu.make_async_copy(
                ys_hbm.at[g, pl.ds(src_row, size), :], ybuf_ref.at[half, sub, pl.ds(dst_row, size), :],
                sem.at[half]), action)

    @pl.when(step == 0)
    def _():
        ybuf_ref[...] = jnp.zeros_like(ybuf_ref)
        fetch(step, "start")

    @pl.when(step + 1 < pl.num_programs(0))
    def _():
        fetch(step + 1, "start")

    fetch(step, "wait")
    half = step % 2
    for sub in range(FIN_SUBS):
        r0 = sub * MIX_SUB
        pos_col = _rows_to_cols(pos_ref[sub])[:, 0:1].astype(jnp.int32)
        sel = (lax.broadcasted_iota(jnp.int32, (MIX_SUB, SORT_ROWS), 1) == pos_col).astype(BF16)
        y = _dot(sel, ybuf_ref[half, sub])
        o_ref[r0:r0 + MIX_SUB, :] = _rms(x1_ref[r0:r0 + MIX_SUB, :] + gt_ref[...] * y, gf_ref[...])


def _final_call(meta, x1, pos, gt2, gf, ys, seq):
    t = x1.shape[0]
    per_b = seq // TM_FIN
    tok = pl.BlockSpec((TM_FIN, D_MODEL), lambda i, meta: (i, 0))
    return pl.pallas_call(
        _final_kernel,
        grid_spec=pltpu.PrefetchScalarGridSpec(
            num_scalar_prefetch=1,
            grid=(t // TM_FIN,),
            in_specs=[tok, pl.BlockSpec((FIN_SUBS, 8, MIX_SUB), lambda i, meta: (i, 0, 0)),
                      pl.BlockSpec((None, 1, D_MODEL), lambda i, meta: (i // per_b, 0, 0)),
                      pl.BlockSpec((1, D_MODEL), lambda i, meta: (0, 0)),
                      pl.BlockSpec(memory_space=pl.ANY)],
            out_specs=tok,
            scratch_shapes=[pltpu.VMEM((2, FIN_SUBS, SORT_ROWS, D_MODEL), BF16), pltpu.SemaphoreType.DMA((2,))]),
        out_shape=jax.ShapeDtypeStruct((t, D_MODEL), F32),
        compiler_params=_params(("arbitrary",)),
        name="final",
    )(meta, x1, pos, gt2, gf, ys)


def kernel(x, c, w_ada, b_ada, norm1_g, w_in, w_gk2, b_gk, gla_norm_g, sink, w_o, norm2_g,
           w_group, b_group, w_router, b_router, w_gate, w_up, w_down, norm_f_g):
    bsz, seq, d = x.shape
    assert w_ada.shape[0] == 1, "single layer: the final norm is fused behind the layer's MoE"
    l = 0
    t = bsz * seq
    n_sub = t // MIX_SUB
    cap = -(-(t + n_sub * (ROW_PACK - 1) + TM_EXP) // TM_EXP) * TM_EXP
    n_work = t // TM_EXP + n_sub * (ROW_PACK - 1) * N_GROUPS // TM_EXP + N_GROUPS + 1

    mod, weights = _mod_call(c, w_ada[l], b_ada[l], w_in[l], w_o[l])
    sh1, sc1, gt1, sh2, sc2, gt2 = [m.reshape(bsz, 1, d) for m in jnp.split(mod, N_MOD, axis=-1)]

    w2p = jnp.pad(w_gk2[l], ((0, LANES - GLA_GATE_RANK), (0, 0))).astype(BF16)
    fill = jnp.full((SWA_HEADS, 2 * WINDOW), -jnp.inf, F32).at[:, 0].set(sink[l])
    n_logit = N_GROUPS + N_EXPERTS
    wrt = jnp.zeros((n_logit, ROUTE_STRIDE, d), F32).at[:, 0, :].set(
        jnp.concatenate([w_group[l], w_router[l]], axis=1).T).reshape(n_logit * ROUTE_STRIDE, d).astype(BF16)
    brt = jnp.zeros((n_logit, ROUTE_STRIDE), F32).at[:, 0].set(
        jnp.concatenate([b_group[l], b_router[l]])).reshape(n_logit * ROUTE_STRIDE, 1)

    x1, hs, pos, meta, (wg, wu, wd) = _mixer_call(
        x.reshape(t, d), (norm1_g[l].reshape(1, d), sc1, sh1, gt1), (norm2_g[l].reshape(1, d), sc2, sh2),
        w2p, b_gk[l].reshape(1, GLA_KEY_W), gla_norm_g[l].reshape(1, GLA_DV), fill, wrt, brt, weights,
        (w_gate[l], w_up[l], w_down[l]), cap, bsz, seq)

    last = meta.reshape(n_sub, META)[-1]
    totals = last[:N_GROUPS] + last[N_GROUPS:]
    ends = jnp.cumsum((totals + TM_EXP - 1) // TM_EXP)
    item = jnp.minimum(jnp.arange(n_work, dtype=jnp.int32), ends[-1] - 1)
    grp = jnp.sum(item[:, None] >= ends[None, :], axis=1).astype(jnp.int32)
    blk = item - jnp.concatenate([jnp.zeros((1,), ends.dtype), ends[:-1]])[grp]
    valid = (jnp.arange(n_work) < ends[-1]).astype(jnp.int32)

    ys = _experts_call(grp, blk.astype(jnp.int32), valid, hs, wg, wu, wd)
    out = _final_call(meta, x1, pos, gt2, norm_f_g.reshape(1, d), ys, seq)
    return out.reshape(bsz, seq, d)
```

```python
import functools

import jax
import jax.numpy as jnp
from jax import lax
from jax.experimental import pallas as pl
from jax.experimental.pallas import tpu as pltpu

F32 = jnp.float32
BF16 = jnp.bfloat16

D_MODEL = 1024
GLA_HEADS = 4
GLA_DK = 128
GLA_DV = 256
GLA_KEY_W = GLA_HEADS * GLA_DK
GLA_VAL_W = GLA_HEADS * GLA_DV
GLA_GATE_RANK = 16
GLA_TAU = 16.0
GLA_CHUNK = 64
SWA_HEADS = 16
SWA_KV_HEADS = 2
SWA_GROUP = SWA_HEADS // SWA_KV_HEADS
SWA_DH = 64
SWA_W = SWA_HEADS * SWA_DH
SWA_KV_W = SWA_KV_HEADS * SWA_DH
WINDOW = 128
N_GROUPS = 4
EXPERTS_PER_GROUP = 4
N_EXPERTS = N_GROUPS * EXPERTS_PER_GROUP
D_EXPERT = 256
N_MOD = 6
EPS = 1e-6
IN_WIDTHS = (GLA_KEY_W, GLA_KEY_W, GLA_VAL_W, GLA_VAL_W, GLA_GATE_RANK,
             SWA_W, SWA_KV_W, SWA_KV_W, D_MODEL, D_MODEL)

LANES = 128
LOG2_E = 1.4426950408889634
LN_2 = 0.6931471805599453
VMEM_LIMIT = 56 * 1024 * 1024

TM_MIX = 512
MIX_SUB = 256
SUBS = TM_MIX // MIX_SUB
TM_FIN = 1024
FIN_SUBS = TM_FIN // MIX_SUB
ROW_PACK = 16
SORT_ROWS = MIX_SUB + 64
SEG_BITS = (256, 128, 64, 32, 16)
HS_W = D_MODEL + LANES
TM_EXP = 1024
META = 2 * N_GROUPS
ROUTE_STRIDE = 8
TN_MOD = 512


def _params(sem):
    return pltpu.CompilerParams(dimension_semantics=sem, vmem_limit_bytes=VMEM_LIMIT)


def _split(a):
    hi = a.astype(BF16)
    lo = (a - hi.astype(F32)).astype(BF16)
    return hi, lo


def _dot(a, b):
    return jnp.dot(a, b, preferred_element_type=F32)


def _dot_nt(a, b):
    return lax.dot_general(a, b, (((1,), (1,)), ((), ())), preferred_element_type=F32)


def _dot_tn(a, b):
    return lax.dot_general(a, b, (((0,), (0,)), ((), ())), preferred_element_type=F32)


def _dot3(a, b):
    a_hi, a_lo = _split(a)
    b_hi, b_lo = _split(b)
    return _dot(a_hi, b_hi) + _dot(a_hi, b_lo) + _dot(a_lo, b_hi)


def _sigmoid(x):
    return 1.0 / (1.0 + jnp.exp(-x))


def _silu(x):
    return x * _sigmoid(x)


def _rms(x, g):
    return x * lax.rsqrt(jnp.mean(x * x, axis=-1, keepdims=True) + EPS) * g


def _first_of(vals, target):
    idx = jnp.full(target.shape, len(vals) - 1, jnp.int32)
    for j in range(len(vals) - 2, -1, -1):
        idx = jnp.where(vals[j] == target, j, idx)
    return idx


def _rows_to_cols(a):
    return jnp.concatenate([jnp.transpose(a[:, i * LANES:(i + 1) * LANES]) for i in range(a.shape[1] // LANES)],
                           axis=0)


def _segment_copies(pcs, src_rows, dst_rows, make_copy, action):
    for g in range(N_GROUPS):
        for bit in SEG_BITS:
            done = pcs[g] & ~(2 * bit - 1)

            @pl.when((pcs[g] & bit) != 0)
            def _(g=g, bit=bit, done=done):
                cp = make_copy(g, pl.multiple_of(src_rows[g] + done, ROW_PACK),
                               pl.multiple_of(dst_rows[g] + done, ROW_PACK), bit)
                cp.start() if action == "start" else cp.wait()


def _mod_kernel(c_ref, w_ref, b_ref, o_ref):
    o_ref[...] = _dot3(_silu(c_ref[...]), w_ref[...]) + b_ref[...]


def _mod_call(c, w_ada, b_ada):
    bsz = c.shape[0]
    n = w_ada.shape[1]
    return pl.pallas_call(
        _mod_kernel,
        grid=(n // TN_MOD,),
        in_specs=[pl.BlockSpec((bsz, D_MODEL), lambda j: (0, 0)),
                  pl.BlockSpec((D_MODEL, TN_MOD), lambda j: (0, j)),
                  pl.BlockSpec((1, TN_MOD), lambda j: (0, j))],
        out_specs=pl.BlockSpec((bsz, TN_MOD), lambda j: (0, j)),
        out_shape=jax.ShapeDtypeStruct((bsz, n), F32),
        compiler_params=_params(("parallel",)),
        name="mod",
    )(c, w_ada, b_ada.reshape(1, n))


QA, KA, VA, RA, GATE, QB, KB, VB, GA, GB, WO = range(11)
N_WEIGHTS = 11
N_VMEM_IN = 18


def _mixer_kernel(x_ref, g1_ref, sc1_ref, sh1_ref, gt1_ref, w2_ref, bgk_ref, tri_ref, gn_ref, fill_ref,
                  g2_ref, sc2_ref, sh2_ref, wrt_ref, brt_ref, upper_ref, wgf_ref, wuf_ref, wdf_ref, *rest):
    w_hbm = rest[:N_WEIGHTS]
    x1_ref, hs_hbm, pos_hbm, meta_ref, wgb_ref, wub_ref, wdb_ref = rest[N_WEIGHTS:N_WEIGHTS + 7]
    w = rest[N_WEIGHTS + 7:2 * N_WEIGHTS + 7]
    (st_ref, kprev_ref, vprev_ref, oa_ref, ob_ref, stage_ref, pstage_ref, hkeep_ref, zero_ref, base_ref,
     sem) = rest[2 * N_WEIGHTS + 7:]
    first = pl.program_id(1) == 0
    step = pl.program_id(0) * pl.num_programs(1) + pl.program_id(1)
    n_steps = pl.num_programs(0) * pl.num_programs(1)

    @pl.when(step == 0)
    def _():
        for src, dst in zip(w_hbm, w):
            pltpu.sync_copy(src, dst)
        for g in range(N_GROUPS):
            base_ref[g] = 0
        hkeep_ref[...] = jnp.zeros_like(hkeep_ref)

    @pl.when(first)
    def _():
        st_ref[...] = jnp.zeros_like(st_ref)
        kprev_ref[...] = jnp.zeros_like(kprev_ref)
        vprev_ref[...] = jnp.zeros_like(vprev_ref)

    wgb_ref[...] = wgf_ref[...].astype(BF16)
    wub_ref[...] = wuf_ref[...].astype(BF16)
    wdb_ref[...] = wdf_ref[...].astype(BF16)

    def stage_copy(slot):
        return lambda g, src_row, dst_row, size: pltpu.make_async_copy(
            stage_ref.at[slot, pl.ds(src_row, size), :], hs_hbm.at[g, pl.ds(dst_row, size), :], sem.at[slot])

    pos_copy = lambda slot, tile: pltpu.make_async_copy(pstage_ref.at[slot], pos_hbm.at[tile], sem.at[slot])
    zeros4 = [0] * N_GROUPS

    def publish(sub, owner, routed, drain):
        sorted_rows, pos, pcs, offs = routed
        tile = owner * SUBS + sub
        m0 = tile * META

        @pl.when(owner > 0)
        def _():
            prev = [meta_ref[m0 - SUBS * META + N_GROUPS + g] for g in range(N_GROUPS)]
            _segment_copies(prev, zeros4, zeros4, stage_copy(sub), "wait")
            pos_copy(sub, tile).wait()

        stage_ref[sub] = sorted_rows
        pstage_ref[sub] = jnp.broadcast_to(pos, (8, MIX_SUB))
        bases = [base_ref[g] for g in range(N_GROUPS)]
        for g in range(N_GROUPS):
            meta_ref[m0 + g] = bases[g]
            meta_ref[m0 + N_GROUPS + g] = pcs[g]
            base_ref[g] = bases[g] + pcs[g]
        _segment_copies(pcs, offs, bases, stage_copy(sub), "start")
        pos_copy(sub, tile).start()

        def wait_own():
            _segment_copies(pcs, zeros4, zeros4, stage_copy(sub), "wait")
            pos_copy(sub, tile).wait()

        if drain is True:
            wait_own()
        elif drain is not False:
            pl.when(drain)(wait_own)

    route = lambda hb: _route_and_sort(hb, wrt_ref, brt_ref, upper_ref)
    moe_in = lambda x1: (_rms(x1, g2_ref[...]) * (1.0 + sc2_ref[...]) + sh2_ref[...]).astype(BF16)
    last_step = step == n_steps - 1

    routing, results = route(hkeep_ref[...]), []
    for sub in range(SUBS):
        r0 = sub * MIX_SUB
        x1, routed = _mixer_tile(x_ref[r0:r0 + MIX_SUB, :], first if sub == 0 else False, g1_ref, sc1_ref,
                                 sh1_ref, gt1_ref, w2_ref, bgk_ref, tri_ref, gn_ref, fill_ref, w, st_ref,
                                 kprev_ref, vprev_ref, oa_ref, ob_ref, side=routing)
        x1_ref[r0:r0 + MIX_SUB, :] = x1
        results.append(routed)
        if sub + 1 < SUBS:
            routing = route(moe_in(x1))
        else:
            hkeep_ref[...] = moe_in(x1)

    @pl.when(step > 0)
    def _():
        publish(SUBS - 1, step - 1, results[0], False)

    for sub in range(1, SUBS):
        publish(sub - 1, step, results[sub], last_step)

    @pl.when(last_step)
    def _():
        routing = route(hkeep_ref[...])
        next(routing)
        next(routing)
        publish(SUBS - 1, step, next(routing), True)
        zero_ref[...] = jnp.zeros_like(zero_ref)
        tails = [pltpu.make_async_copy(zero_ref, hs_hbm.at[g, pl.ds(pl.multiple_of(base_ref[g], ROW_PACK), TM_EXP), :],
                                       sem.at[0]) for g in range(N_GROUPS)]
        for cp in tails:
            cp.start()
        for cp in tails:
            cp.wait()


def _mixer_tile(x, first, g1_ref, sc_ref, sh_ref, gt_ref, w2_ref, bgk_ref, tri_ref, gn_ref, fill_ref,
                w, st_ref, kprev_ref, vprev_ref, oa_ref, ob_ref, side=None):
    ts = MIX_SUB
    advance = (lambda: next(side)) if side is not None else (lambda: None)
    hb = (_rms(x, g1_ref[...]) * (1.0 + sc_ref[...]) + sh_ref[...]).astype(BF16)

    z = _dot(hb, w[GATE][...])
    gz = _dot(z.astype(BF16), w2_ref[...]) + bgk_ref[...]
    soft = jnp.log2(1.0 + jnp.exp2(jnp.abs(gz) * (-LOG2_E)))
    la_hi, la_lo = _split(jnp.minimum(gz, 0.0) * (1.0 / GLA_TAU) - soft * (LN_2 / GLA_TAU))
    qa = _dot(hb, w[QA][...])
    ka = _dot(hb, w[KA][...])
    advance()
    va = _dot(hb, w[VA][...]).astype(BF16)
    bc = _dot(tri_ref[...], la_hi) + _dot(tri_ref[...], la_lo)
    ra = _dot(hb, w[RA][...])
    advance()
    qb = _dot(hb, w[QB][...]).astype(BF16) * jnp.asarray(SWA_DH ** -0.5, BF16)
    kb = _dot(hb, w[KB][...]).astype(BF16)
    vb = _dot(hb, w[VB][...]).astype(BF16)

    nc = ts // GLA_CHUNK
    chunk = lambda c: slice(c * GLA_CHUNK, (c + 1) * GLA_CHUNK)
    btot = [bc[(c + 1) * GLA_CHUNK - 1:(c + 1) * GLA_CHUNK, :] for c in range(nc)]
    pre = [jnp.zeros_like(btot[0])]
    for c in range(nc):
        pre.append(pre[c] + btot[c])
    rows = lambda vecs: jnp.concatenate([jnp.broadcast_to(v, (GLA_CHUNK, GLA_KEY_W)) for v in vecs], axis=0)
    qf = qa * (GLA_DK ** -0.5) * jnp.exp(bc)
    q_dec = qf.astype(BF16)
    q_in = (qf * rows([jnp.exp(pre[c]) for c in range(nc)])).astype(BF16)
    k_inv = (ka * jnp.exp(-bc)).astype(BF16)
    k_end = ka * jnp.exp(rows(btot) - bc)
    k_tile = (k_end * rows([jnp.exp(pre[nc] - pre[c + 1]) for c in range(nc)])).astype(BF16)
    k_cross = {(c, c2): (k_end[chunk(c2)] * jnp.exp(pre[c] - pre[c2 + 1])).astype(BF16)
               for c in range(nc) for c2 in range(c)}
    dec_tile = jnp.exp(pre[nc])
    causal = (lax.broadcasted_iota(jnp.int32, (ts, ts), 1) <= lax.broadcasted_iota(jnp.int32, (ts, ts), 0))

    def gla_scores(h):
        ks = slice(h * GLA_DK, (h + 1) * GLA_DK)
        att = []
        for c in range(nc):
            keys = [k_cross[(c, c2)][:, ks] for c2 in range(c)] + [k_inv[chunk(c), ks]]
            keys += [jnp.zeros((GLA_CHUNK, GLA_DK), BF16)] * (nc - 1 - c)
            att.append(_dot_nt(q_dec[chunk(c), ks], jnp.concatenate(keys, axis=0)))
        return jnp.where(causal, jnp.concatenate(att, axis=0), 0.0).astype(BF16)

    atts = [gla_scores(h) for h in range(GLA_HEADS)]
    gn = gn_ref[...]
    for h in range(GLA_HEADS):
        ks = slice(h * GLA_DK, (h + 1) * GLA_DK)
        vs = slice(h * GLA_DV, (h + 1) * GLA_DV)
        st = st_ref[h]
        o = _dot(atts[h], va[:, vs]) + _dot(q_in[:, ks], st.astype(BF16))
        dec_col = jnp.transpose(jnp.broadcast_to(dec_tile[:, ks], (8, GLA_DK)))[:, 0:1]
        st_ref[h] = st * dec_col + _dot_tn(k_tile[:, ks], va[:, vs])
        o = o * lax.rsqrt(jnp.mean(o * o, axis=-1, keepdims=True) + EPS) * gn
        oa_ref[:, vs] = o * _silu(ra[:, vs])

    wd = WINDOW
    k_all = jnp.concatenate([kprev_ref[...], kb], axis=0)
    v_all = jnp.concatenate([vprev_ref[...], vb], axis=0)
    kprev_ref[...] = kb[ts - wd:, :]
    vprev_ref[...] = vb[ts - wd:, :]
    lo_half = lax.broadcasted_iota(jnp.int32, (wd + ts, LANES), 1) < SWA_DH
    k_sw = pltpu.roll(k_all, SWA_DH, 1)
    v_sw = pltpu.roll(v_all, SWA_DH, 1)
    k2 = (jnp.where(lo_half, k_all, k_sw), jnp.where(lo_half, k_sw, k_all))
    v2 = (jnp.where(lo_half, v_all, v_sw), jnp.where(lo_half, v_sw, v_all))
    qi = lax.broadcasted_iota(jnp.int32, (wd, 2 * wd), 0) + wd
    kj = lax.broadcasted_iota(jnp.int32, (wd, 2 * wd), 1)
    band = (qi - kj >= 0) & (qi - kj < wd)
    band_first = band & (jnp.logical_not(first) | (kj >= wd)) if first is not False else band
    q_lo = lax.broadcasted_iota(jnp.int32, (wd, LANES), 1) < SWA_DH
    row0 = lax.broadcasted_iota(jnp.int32, (2 * wd, LANES), 0) == 0
    ones = jnp.ones((2 * wd, LANES), BF16)
    half = SWA_GROUP // 2
    combos = [(blk, hk) for blk in range(ts // wd) for hk in range(SWA_KV_HEADS)]

    def swa_probs(blk, hk):
        parts = []
        for p in range(half):
            qp = qb[blk * wd:(blk + 1) * wd, (hk * half + p) * LANES:(hk * half + p + 1) * LANES]
            parts += [jnp.where(q_lo, qp, jnp.zeros_like(qp)), jnp.where(q_lo, jnp.zeros_like(qp), qp)]
        s = _dot_nt(jnp.concatenate(parts, axis=0), k2[hk][blk * wd:blk * wd + 2 * wd])
        valid = band_first if blk == 0 else band
        es = []
        for r in range(SWA_GROUP):
            head = hk * SWA_GROUP + r
            sh = jnp.where(valid, s[r * wd:(r + 1) * wd], fill_ref[head:head + 1, :])
            es.append(jnp.exp(sh - jnp.max(sh, axis=-1, keepdims=True)).astype(BF16))
        return jnp.concatenate(es, axis=0)

    probs = [swa_probs(blk, hk) for blk, hk in combos]
    side_out = advance()
    ga = _dot(hb, w[GA][...])
    gb = _dot(hb, w[GB][...])
    for (blk, hk), e in zip(combos, probs):
        vv = jnp.where(row0, jnp.zeros_like(ones), v2[hk][blk * wd:blk * wd + 2 * wd])
        o = _dot(e, jnp.concatenate([vv, ones], axis=1))
        for p in range(half):
            oe = o[(2 * p) * wd:(2 * p + 1) * wd]
            oo = o[(2 * p + 1) * wd:(2 * p + 2) * wd]
            num = jnp.where(q_lo, oe[:, :LANES], oo[:, :LANES])
            den = jnp.where(q_lo, oe[:, LANES:], oo[:, LANES:])
            ob_ref[blk * wd:(blk + 1) * wd, (hk * half + p) * LANES:(hk * half + p + 1) * LANES] = num / den

    merged = _sigmoid(ga) * oa_ref[...] + _sigmoid(gb) * ob_ref[...]
    return x + gt_ref[...] * _dot(merged.astype(BF16), w[WO][...]), side_out


def _route_and_sort(hb, wrt_ref, brt_ref, upper_ref):
    tm = hb.shape[0]
    neg = -jnp.inf
    lt = _dot_nt(wrt_ref[...], hb) + brt_ref[...]
    yield None
    lrow = lambda k: lt[ROUTE_STRIDE * k:ROUTE_STRIDE * k + 1, :]
    lg = [lrow(g) for g in range(N_GROUPS)]
    gmax = functools.reduce(jnp.maximum, lg)
    g_w = 1.0 / functools.reduce(jnp.add, [jnp.exp(v - gmax) for v in lg])
    g_idx = _first_of(lg, gmax)
    le = []
    for j in range(EXPERTS_PER_GROUP):
        v = lrow(N_GROUPS + (N_GROUPS - 1) * EXPERTS_PER_GROUP + j)
        for g in range(N_GROUPS - 2, -1, -1):
            v = jnp.where(g_idx == g, lrow(N_GROUPS + g * EXPERTS_PER_GROUP + j), v)
        le.append(v)
    v1 = functools.reduce(jnp.maximum, le)
    i1 = _first_of(le, v1)
    le2 = [jnp.where(i1 == j, neg, le[j]) for j in range(EXPERTS_PER_GROUP)]
    v2 = functools.reduce(jnp.maximum, le2)
    i2 = _first_of(le2, v2)
    e2 = jnp.exp(v2 - v1)
    w1 = g_w / (1.0 + e2)
    w2 = g_w * e2 / (1.0 + e2)

    sub = lax.broadcasted_iota(jnp.int32, (ROW_PACK, tm), 0)
    rank = _dot((sub == g_idx).astype(BF16), upper_ref[...])
    yield None
    pcs, offs, run = [], [], 0
    pos = jnp.zeros((1, tm), F32)
    for g in range(N_GROUPS):
        mine = g_idx == g
        cnt = jnp.sum(mine.astype(F32), axis=1, keepdims=True)[0, 0].astype(jnp.int32)
        pcs.append(((cnt + (ROW_PACK - 1)) // ROW_PACK) * ROW_PACK)
        offs.append(run)
        pos = pos + jnp.where(mine, rank[g:g + 1, :] + jnp.asarray(run, F32), 0.0)
        run = run + pcs[g]
    perm = (lax.broadcasted_iota(jnp.int32, (SORT_ROWS, tm), 0) == pos.astype(jnp.int32)).astype(BF16)
    cw = jnp.where(sub == i1, w1, 0.0) + jnp.where(sub == i2, w2, 0.0)
    cw_hi = cw.astype(BF16).astype(F32)
    packed = cw_hi + pltpu.roll(cw - cw_hi, EXPERTS_PER_GROUP, 0)
    cw_cols = _rows_to_cols(jnp.concatenate([packed, jnp.zeros((LANES - ROW_PACK, tm), F32)], axis=0))
    sorted_rows = _dot(perm, jnp.concatenate([hb, cw_cols.astype(BF16)], axis=1)).astype(BF16)
    yield sorted_rows, pos, pcs, offs


def _mixer_call(x2, vecs1, vecs2, w2p, bgk, gn, fill, wrt, brt, weights, experts_f32, cap, bsz, seq):
    g1, sc1, sh1, gt1 = vecs1
    g2, sc2, sh2 = vecs2
    ns = seq // TM_MIX
    n_steps = bsz * ns
    tok = pl.BlockSpec((TM_MIX, D_MODEL), lambda b, s: (b * ns + s, 0))
    vec = pl.BlockSpec((None, 1, D_MODEL), lambda b, s: (b, 0, 0))
    full = lambda a: pl.BlockSpec(a.shape, lambda b, s: (0,) * a.ndim)
    hbm = pl.BlockSpec(memory_space=pl.ANY)
    slab = lambda a: pl.BlockSpec((a.shape[0] // n_steps, a.shape[1]), lambda b, s: (b * ns + s, 0))
    idx = jnp.arange(MIX_SUB)
    tri = ((idx[:, None] // GLA_CHUNK == idx[None, :] // GLA_CHUNK) & (idx[None, :] <= idx[:, None])).astype(BF16)
    upper = (idx[:, None] < idx[None, :]).astype(BF16)
    ex2d = [e.reshape(-1, e.shape[-1]) for e in experts_f32]
    n_sub = n_steps * SUBS
    vmem_in = [x2, g1, sc1, sh1, gt1, w2p, bgk, tri, gn, fill, g2, sc2, sh2, wrt, brt, upper]
    assert len(vmem_in) + len(ex2d) == N_VMEM_IN + 1
    outs = pl.pallas_call(
        _mixer_kernel,
        grid=(bsz, ns),
        in_specs=[tok, full(g1), vec, vec, vec, full(w2p), full(bgk), full(tri), full(gn), full(fill),
                  full(g2), vec, vec, full(wrt), full(brt), full(upper)]
        + [slab(e) for e in ex2d] + [hbm] * N_WEIGHTS,
        out_specs=[tok, hbm, hbm, pl.BlockSpec(memory_space=pltpu.SMEM)] + [slab(e) for e in ex2d],
        out_shape=[jax.ShapeDtypeStruct(x2.shape, F32),
                   jax.ShapeDtypeStruct((N_GROUPS, cap, HS_W), BF16),
                   jax.ShapeDtypeStruct((n_sub, 8, MIX_SUB), F32),
                   jax.ShapeDtypeStruct((n_sub * META,), jnp.int32)]
        + [jax.ShapeDtypeStruct(e.shape, BF16) for e in ex2d],
        scratch_shapes=[pltpu.VMEM(wt.shape, BF16) for wt in weights]
        + [pltpu.VMEM((GLA_HEADS, GLA_DK, GLA_DV), F32),
           pltpu.VMEM((WINDOW, SWA_KV_W), BF16), pltpu.VMEM((WINDOW, SWA_KV_W), BF16),
           pltpu.VMEM((MIX_SUB, GLA_VAL_W), F32), pltpu.VMEM((MIX_SUB, SWA_W), F32),
           pltpu.VMEM((SUBS, SORT_ROWS, HS_W), BF16), pltpu.VMEM((SUBS, 8, MIX_SUB), F32),
           pltpu.VMEM((MIX_SUB, D_MODEL), BF16), pltpu.VMEM((TM_EXP, HS_W), BF16),
           pltpu.SMEM((N_GROUPS,), jnp.int32), pltpu.SemaphoreType.DMA((SUBS,))],
        compiler_params=_params(("arbitrary", "arbitrary")),
        name="mixer",
    )(*vmem_in, *ex2d, *weights)
    x1, hs, pos, meta = outs[:4]
    experts_bf16 = [o.reshape(e.shape) for o, e in zip(outs[4:], experts_f32)]
    return x1, hs, pos, meta, experts_bf16


def _experts_kernel(grp_ref, blk_ref, valid_ref, hs_ref, wg_ref, wu_ref, wd_ref, ys_ref):
    @pl.when(valid_ref[pl.program_id(0)] != 0)
    def _():
        hx = hs_ref[...]
        h = hx[:, :D_MODEL]
        cw = hx[:, D_MODEL:].astype(F32)
        gates = [_dot(h, wg_ref[j]) for j in range(EXPERTS_PER_GROUP)]
        ups = [_dot(h, wu_ref[j]) for j in range(EXPERTS_PER_GROUP)]
        hids = [(_silu(gates[j]) * ups[j]
                 * (cw[:, j:j + 1] + cw[:, EXPERTS_PER_GROUP + j:EXPERTS_PER_GROUP + j + 1])).astype(BF16)
                for j in range(EXPERTS_PER_GROUP)]
        y = _dot(hids[0], wd_ref[0])
        for j in range(1, EXPERTS_PER_GROUP):
            y = y + _dot(hids[j], wd_ref[j])
        ys_ref[...] = y.astype(BF16)


def _experts_call(grp, blk, valid, hs, wg, wu, wd):
    n_work = grp.shape[0]
    rows = lambda width: pl.BlockSpec((None, TM_EXP, width), lambda i, grp, blk, valid: (grp[i], blk[i], 0))
    wspec = lambda a: pl.BlockSpec((EXPERTS_PER_GROUP,) + a.shape[1:], lambda i, grp, blk, valid: (grp[i], 0, 0))
    return pl.pallas_call(
        _experts_kernel,
        grid_spec=pltpu.PrefetchScalarGridSpec(
            num_scalar_prefetch=3,
            grid=(n_work,),
            in_specs=[rows(HS_W), wspec(wg), wspec(wu), wspec(wd)],
            out_specs=rows(D_MODEL)),
        out_shape=jax.ShapeDtypeStruct(hs.shape[:2] + (D_MODEL,), BF16),
        compiler_params=_params(("arbitrary",)),
        name="experts",
    )(grp, blk, valid, hs, wg, wu, wd)


def _final_kernel(meta_ref, x1_ref, pos_ref, gt_ref, gf_ref, ys_hbm, o_ref, ybuf_ref, sem):
    step = pl.program_id(0)

    def fetch(s, action):
        half = s % 2
        for sub in range(FIN_SUBS):
            m0 = (s * FIN_SUBS + sub) * META
            bases = [meta_ref[m0 + g] for g in range(N_GROUPS)]
            pcs = [meta_ref[m0 + N_GROUPS + g] for g in range(N_GROUPS)]
            offs, run = [], 0
            for g in range(N_GROUPS):
                offs.append(run)
                run = run + pcs[g]
            _segment_copies(pcs, bases, offs, lambda g, src_row, dst_row, size, sub=sub: pltpu.make_async_copy(
                ys_hbm.at[g, pl.ds(src_row, size), :], ybuf_ref.at[half, sub, pl.ds(dst_row, size), :],
                sem.at[half]), action)

    @pl.when(step == 0)
    def _():
        ybuf_ref[...] = jnp.zeros_like(ybuf_ref)
        fetch(step, "start")

    @pl.when(step + 1 < pl.num_programs(0))
    def _():
        fetch(step + 1, "start")

    fetch(step, "wait")
    half = step % 2
    for sub in range(FIN_SUBS):
        r0 = sub * MIX_SUB
        pos_col = _rows_to_cols(pos_ref[sub])[:, 0:1].astype(jnp.int32)
        sel = (lax.broadcasted_iota(jnp.int32, (MIX_SUB, SORT_ROWS), 1) == pos_col).astype(BF16)
        y = _dot(sel, ybuf_ref[half, sub])
        o_ref[r0:r0 + MIX_SUB, :] = _rms(x1_ref[r0:r0 + MIX_SUB, :] + gt_ref[...] * y, gf_ref[...])


def _final_call(meta, x1, pos, gt2, gf, ys, seq):
    t = x1.shape[0]
    per_b = seq // TM_FIN
    tok = pl.BlockSpec((TM_FIN, D_MODEL), lambda i, meta: (i, 0))
    return pl.pallas_call(
        _final_kernel,
        grid_spec=pltpu.PrefetchScalarGridSpec(
            num_scalar_prefetch=1,
            grid=(t // TM_FIN,),
            in_specs=[tok, pl.BlockSpec((FIN_SUBS, 8, MIX_SUB), lambda i, meta: (i, 0, 0)),
                      pl.BlockSpec((None, 1, D_MODEL), lambda i, meta: (i // per_b, 0, 0)),
                      pl.BlockSpec((1, D_MODEL), lambda i, meta: (0, 0)),
                      pl.BlockSpec(memory_space=pl.ANY)],
            out_specs=tok,
            scratch_shapes=[pltpu.VMEM((2, FIN_SUBS, SORT_ROWS, D_MODEL), BF16), pltpu.SemaphoreType.DMA((2,))]),
        out_shape=jax.ShapeDtypeStruct((t, D_MODEL), F32),
        compiler_params=_params(("arbitrary",)),
        name="final",
    )(meta, x1, pos, gt2, gf, ys)


def kernel(x, c, w_ada, b_ada, norm1_g, w_in, w_gk2, b_gk, gla_norm_g, sink, w_o, norm2_g,
           w_group, b_group, w_router, b_router, w_gate, w_up, w_down, norm_f_g):
    bsz, seq, d = x.shape
    assert w_ada.shape[0] == 1, "single layer: the final norm is fused behind the layer's MoE"
    l = 0
    t = bsz * seq
    n_sub = t // MIX_SUB
    cap = -(-(t + n_sub * (ROW_PACK - 1) + TM_EXP) // TM_EXP) * TM_EXP
    n_work = t // TM_EXP + n_sub * (ROW_PACK - 1) * N_GROUPS // TM_EXP + N_GROUPS + 1

    mod = _mod_call(c, w_ada[l], b_ada[l])
    sh1, sc1, gt1, sh2, sc2, gt2 = [m.reshape(bsz, 1, d) for m in jnp.split(mod, N_MOD, axis=-1)]

    w_parts = _split_cols(w_in[l])
    w_parts[GATE] = jnp.pad(w_parts[GATE], ((0, 0), (0, LANES - GLA_GATE_RANK)))
    weights = [wt.astype(BF16) for wt in w_parts + [w_o[l]]]
    w2p = jnp.pad(w_gk2[l], ((0, LANES - GLA_GATE_RANK), (0, 0))).astype(BF16)
    fill = jnp.full((SWA_HEADS, 2 * WINDOW), -jnp.inf, F32).at[:, 0].set(sink[l])
    n_logit = N_GROUPS + N_EXPERTS
    wrt = jnp.zeros((n_logit, ROUTE_STRIDE, d), F32).at[:, 0, :].set(
        jnp.concatenate([w_group[l], w_router[l]], axis=1).T).reshape(n_logit * ROUTE_STRIDE, d).astype(BF16)
    brt = jnp.zeros((n_logit, ROUTE_STRIDE), F32).at[:, 0].set(
        jnp.concatenate([b_group[l], b_router[l]])).reshape(n_logit * ROUTE_STRIDE, 1)

    x1, hs, pos, meta, (wg, wu, wd) = _mixer_call(
        x.reshape(t, d), (norm1_g[l].reshape(1, d), sc1, sh1, gt1), (norm2_g[l].reshape(1, d), sc2, sh2),
        w2p, b_gk[l].reshape(1, GLA_KEY_W), gla_norm_g[l].reshape(1, GLA_DV), fill, wrt, brt, weights,
        (w_gate[l], w_up[l], w_down[l]), cap, bsz, seq)

    last = meta.reshape(n_sub, META)[-1]
    totals = last[:N_GROUPS] + last[N_GROUPS:]
    ends = jnp.cumsum((totals + TM_EXP - 1) // TM_EXP)
    item = jnp.minimum(jnp.arange(n_work, dtype=jnp.int32), ends[-1] - 1)
    grp = jnp.sum(item[:, None] >= ends[None, :], axis=1).astype(jnp.int32)
    blk = item - jnp.concatenate([jnp.zeros((1,), ends.dtype), ends[:-1]])[grp]
    valid = (jnp.arange(n_work) < ends[-1]).astype(jnp.int32)

    ys = _experts_call(grp, blk.astype(jnp.int32), valid, hs, wg, wu, wd)
    out = _final_call(meta, x1, pos, gt2, norm_f_g.reshape(1, d), ys, seq)
    return out.reshape(bsz, seq, d)


def _split_cols(w):
    parts, start = [], 0
    for width in IN_WIDTHS:
        parts.append(w[:, start:start + width])
        start += width
    return parts
```

```python
import functools

import jax
import jax.numpy as jnp
from jax import lax
from jax.experimental import pallas as pl
from jax.experimental.pallas import tpu as pltpu

F32 = jnp.float32
BF16 = jnp.bfloat16

D_MODEL = 1024
GLA_HEADS = 4
GLA_DK = 128
GLA_DV = 256
GLA_KEY_W = GLA_HEADS * GLA_DK
GLA_VAL_W = GLA_HEADS * GLA_DV
GLA_GATE_RANK = 16
GLA_TAU = 16.0
GLA_CHUNK = 64
SWA_HEADS = 16
SWA_KV_HEADS = 2
SWA_GROUP = SWA_HEADS // SWA_KV_HEADS
SWA_DH = 64
SWA_W = SWA_HEADS * SWA_DH
SWA_KV_W = SWA_KV_HEADS * SWA_DH
WINDOW = 128
N_GROUPS = 4
EXPERTS_PER_GROUP = 4
N_EXPERTS = N_GROUPS * EXPERTS_PER_GROUP
D_EXPERT = 256
N_MOD = 6
EPS = 1e-6
IN_WIDTHS = (GLA_KEY_W, GLA_KEY_W, GLA_VAL_W, GLA_VAL_W, GLA_GATE_RANK,
             SWA_W, SWA_KV_W, SWA_KV_W, D_MODEL, D_MODEL)

LANES = 128
LOG2_E = 1.4426950408889634
LN_2 = 0.6931471805599453
VMEM_LIMIT = 56 * 1024 * 1024

TM_MIX = 512
MIX_SUB = 256
SUBS = TM_MIX // MIX_SUB
TM_FIN = 1024
FIN_SUBS = TM_FIN // MIX_SUB
ROW_PACK = 16
SORT_ROWS = MIX_SUB + 64
SEG_BITS = (256, 128, 64, 32, 16)
HS_W = D_MODEL + LANES
TM_EXP = 1024
META = 2 * N_GROUPS
SUBLANES = 8
ROUTE_STRIDE = SUBLANES
TN_MOD = 512


def _params(sem):
    return pltpu.CompilerParams(dimension_semantics=sem, vmem_limit_bytes=VMEM_LIMIT)


def _split(a):
    hi = a.astype(BF16)
    lo = (a - hi.astype(F32)).astype(BF16)
    return hi, lo


def _dot(a, b):
    return jnp.dot(a, b, preferred_element_type=F32)


def _dot_nt(a, b):
    return lax.dot_general(a, b, (((1,), (1,)), ((), ())), preferred_element_type=F32)


def _dot_tn(a, b):
    return lax.dot_general(a, b, (((0,), (0,)), ((), ())), preferred_element_type=F32)


def _dot3(a, b):
    a_hi, a_lo = _split(a)
    b_hi, b_lo = _split(b)
    return _dot(a_hi, b_hi) + _dot(a_hi, b_lo) + _dot(a_lo, b_hi)


def _sigmoid(x):
    return 1.0 / (1.0 + jnp.exp(-x))


def _silu(x):
    return x * _sigmoid(x)


def _rms(x, g):
    return x * lax.rsqrt(jnp.mean(x * x, axis=-1, keepdims=True) + EPS) * g


def _first_of(vals, target):
    idx = jnp.full(target.shape, len(vals) - 1, jnp.int32)
    for j in range(len(vals) - 2, -1, -1):
        idx = jnp.where(vals[j] == target, j, idx)
    return idx


def _rows_to_cols(a):
    return jnp.concatenate([jnp.transpose(a[:, i * LANES:(i + 1) * LANES]) for i in range(a.shape[1] // LANES)],
                           axis=0)


def _segment_copies(pcs, src_rows, dst_rows, make_copy, action):
    for g in range(N_GROUPS):
        for bit in SEG_BITS:
            done = pcs[g] & ~(2 * bit - 1)

            @pl.when((pcs[g] & bit) != 0)
            def _(g=g, bit=bit, done=done):
                cp = make_copy(g, pl.multiple_of(src_rows[g] + done, ROW_PACK),
                               pl.multiple_of(dst_rows[g] + done, ROW_PACK), bit)
                cp.start() if action == "start" else cp.wait()


def _mod_kernel(c_ref, w_ref, b_ref, o_ref):
    o_ref[...] = _dot3(_silu(c_ref[...]), w_ref[...]) + b_ref[...]


def _mod_call(c, w_ada, b_ada):
    bsz = c.shape[0]
    n = w_ada.shape[1]
    return pl.pallas_call(
        _mod_kernel,
        grid=(n // TN_MOD,),
        in_specs=[pl.BlockSpec((bsz, D_MODEL), lambda j: (0, 0)),
                  pl.BlockSpec((D_MODEL, TN_MOD), lambda j: (0, j)),
                  pl.BlockSpec((1, TN_MOD), lambda j: (0, j))],
        out_specs=pl.BlockSpec((bsz, TN_MOD), lambda j: (0, j)),
        out_shape=jax.ShapeDtypeStruct((bsz, n), F32),
        compiler_params=_params(("parallel",)),
        name="mod",
    )(c, w_ada, b_ada.reshape(1, n))


QA, KA, VA, RA, GATE, QB, KB, VB, GA, GB, WO = range(11)
N_WEIGHTS = 11


def _mixer_kernel(x_ref, g1_ref, sc1_ref, sh1_ref, gt1_ref, w2_ref, bgk_ref, tri_ref, gn_ref, fill_ref,
                  g2_ref, sc2_ref, sh2_ref, wrt_ref, brt_ref, upper_ref, wgf_ref, wuf_ref, wdf_ref, *rest):
    w_hbm = rest[:N_WEIGHTS]
    x1_ref, hs_hbm, pos_hbm, meta_ref, wgb_ref, wub_ref, wdb_ref = rest[N_WEIGHTS:N_WEIGHTS + 7]
    w = rest[N_WEIGHTS + 7:2 * N_WEIGHTS + 7]
    (st_ref, kprev_ref, vprev_ref, oa_ref, ob_ref, stage_ref, pstage_ref, hkeep_ref, zero_ref, base_ref,
     sem) = rest[2 * N_WEIGHTS + 7:]
    first = pl.program_id(1) == 0
    step = pl.program_id(0) * pl.num_programs(1) + pl.program_id(1)
    n_steps = pl.num_programs(0) * pl.num_programs(1)

    @pl.when(step == 0)
    def _():
        for src, dst in zip(w_hbm, w):
            pltpu.sync_copy(src, dst)
        for g in range(N_GROUPS):
            base_ref[g] = 0
        hkeep_ref[...] = jnp.zeros_like(hkeep_ref)

    @pl.when(first)
    def _():
        st_ref[...] = jnp.zeros_like(st_ref)
        kprev_ref[...] = jnp.zeros_like(kprev_ref)
        vprev_ref[...] = jnp.zeros_like(vprev_ref)

    wgb_ref[...] = wgf_ref[...].astype(BF16)
    wub_ref[...] = wuf_ref[...].astype(BF16)
    wdb_ref[...] = wdf_ref[...].astype(BF16)

    def stage_copy(slot):
        return lambda g, src_row, dst_row, size: pltpu.make_async_copy(
            stage_ref.at[slot, pl.ds(src_row, size), :], hs_hbm.at[g, pl.ds(dst_row, size), :], sem.at[slot])

    pos_copy = lambda slot, tile: pltpu.make_async_copy(pstage_ref.at[slot], pos_hbm.at[tile], sem.at[slot])
    zeros4 = [0] * N_GROUPS

    def wait_copies(sub, tile, pcs):
        _segment_copies(pcs, zeros4, zeros4, stage_copy(sub), "wait")
        pos_copy(sub, tile).wait()

    def publish(sub, owner, routed):
        sorted_rows, pos, pcs, offs = routed
        tile = owner * SUBS + sub
        m0 = tile * META

        @pl.when(owner > 0)
        def _():
            wait_copies(sub, tile, [meta_ref[m0 - SUBS * META + N_GROUPS + g] for g in range(N_GROUPS)])

        stage_ref[sub] = sorted_rows
        pstage_ref[sub] = jnp.broadcast_to(pos, (SUBLANES, MIX_SUB))
        bases = [base_ref[g] for g in range(N_GROUPS)]
        for g in range(N_GROUPS):
            meta_ref[m0 + g] = bases[g]
            meta_ref[m0 + N_GROUPS + g] = pcs[g]
            base_ref[g] = bases[g] + pcs[g]
        _segment_copies(pcs, offs, bases, stage_copy(sub), "start")
        pos_copy(sub, tile).start()

    route = lambda hb: _route_and_sort(hb, wrt_ref, brt_ref, upper_ref)
    moe_in = lambda x1: (_rms(x1, g2_ref[...]) * (1.0 + sc2_ref[...]) + sh2_ref[...]).astype(BF16)
    last_step = step == n_steps - 1

    routing, results = route(hkeep_ref[...]), []
    for sub in range(SUBS):
        r0 = sub * MIX_SUB
        x1, routed = _mixer_tile(x_ref[r0:r0 + MIX_SUB, :], first if sub == 0 else False, g1_ref, sc1_ref,
                                 sh1_ref, gt1_ref, w2_ref, bgk_ref, tri_ref, gn_ref, fill_ref, w, st_ref,
                                 kprev_ref, vprev_ref, oa_ref, ob_ref, side=routing)
        x1_ref[r0:r0 + MIX_SUB, :] = x1
        results.append(routed)
        if sub + 1 < SUBS:
            routing = route(moe_in(x1))
        else:
            hkeep_ref[...] = moe_in(x1)

    @pl.when(step > 0)
    def _():
        publish(SUBS - 1, step - 1, results[0])

    for sub in range(1, SUBS):
        publish(sub - 1, step, results[sub])

    @pl.when(last_step)
    def _():
        routing = route(hkeep_ref[...])
        next(routing)
        next(routing)
        routed = next(routing)
        publish(SUBS - 1, step, routed)
        for sub in range(1, SUBS):
            wait_copies(sub - 1, step * SUBS + sub - 1, results[sub][2])
        wait_copies(SUBS - 1, step * SUBS + SUBS - 1, routed[2])
        zero_ref[...] = jnp.zeros_like(zero_ref)
        tails = [pltpu.make_async_copy(zero_ref, hs_hbm.at[g, pl.ds(pl.multiple_of(base_ref[g], ROW_PACK), TM_EXP), :],
                                       sem.at[0]) for g in range(N_GROUPS)]
        for cp in tails:
            cp.start()
        for cp in tails:
            cp.wait()


def _mixer_tile(x, first, g1_ref, sc_ref, sh_ref, gt_ref, w2_ref, bgk_ref, tri_ref, gn_ref, fill_ref,
                w, st_ref, kprev_ref, vprev_ref, oa_ref, ob_ref, side=None):
    ts = MIX_SUB
    advance = (lambda: next(side)) if side is not None else (lambda: None)
    hb = (_rms(x, g1_ref[...]) * (1.0 + sc_ref[...]) + sh_ref[...]).astype(BF16)

    z = _dot(hb, w[GATE][...])
    gz = _dot(z.astype(BF16), w2_ref[...]) + bgk_ref[...]
    soft = jnp.log2(1.0 + jnp.exp2(jnp.abs(gz) * (-LOG2_E)))
    la_hi, la_lo = _split(jnp.minimum(gz, 0.0) * (1.0 / GLA_TAU) - soft * (LN_2 / GLA_TAU))
    qa = _dot(hb, w[QA][...])
    ka = _dot(hb, w[KA][...])
    advance()
    va = _dot(hb, w[VA][...]).astype(BF16)
    bc = _dot(tri_ref[...], la_hi) + _dot(tri_ref[...], la_lo)
    ra = _dot(hb, w[RA][...])
    advance()
    qb = _dot(hb, w[QB][...]).astype(BF16) * jnp.asarray(SWA_DH ** -0.5, BF16)
    kb = _dot(hb, w[KB][...]).astype(BF16)
    vb = _dot(hb, w[VB][...]).astype(BF16)

    nc = ts // GLA_CHUNK
    chunk = lambda c: slice(c * GLA_CHUNK, (c + 1) * GLA_CHUNK)
    btot = [bc[(c + 1) * GLA_CHUNK - 1:(c + 1) * GLA_CHUNK, :] for c in range(nc)]
    pre = [jnp.zeros_like(btot[0])]
    for c in range(nc):
        pre.append(pre[c] + btot[c])
    rows = lambda vecs: jnp.concatenate([jnp.broadcast_to(v, (GLA_CHUNK, GLA_KEY_W)) for v in vecs], axis=0)
    qf = qa * (GLA_DK ** -0.5) * jnp.exp(bc)
    q_dec = qf.astype(BF16)
    q_in = (qf * rows([jnp.exp(pre[c]) for c in range(nc)])).astype(BF16)
    k_inv = (ka * jnp.exp(-bc)).astype(BF16)
    k_end = ka * jnp.exp(rows(btot) - bc)
    k_tile = (k_end * rows([jnp.exp(pre[nc] - pre[c + 1]) for c in range(nc)])).astype(BF16)
    k_cross = {(c, c2): (k_end[chunk(c2)] * jnp.exp(pre[c] - pre[c2 + 1])).astype(BF16)
               for c in range(nc) for c2 in range(c)}
    dec_tile = jnp.exp(pre[nc])
    causal = (lax.broadcasted_iota(jnp.int32, (ts, ts), 1) <= lax.broadcasted_iota(jnp.int32, (ts, ts), 0))

    def gla_scores(h):
        ks = slice(h * GLA_DK, (h + 1) * GLA_DK)
        att = []
        for c in range(nc):
            keys = [k_cross[(c, c2)][:, ks] for c2 in range(c)] + [k_inv[chunk(c), ks]]
            keys += [jnp.zeros((GLA_CHUNK, GLA_DK), BF16)] * (nc - 1 - c)
            att.append(_dot_nt(q_dec[chunk(c), ks], jnp.concatenate(keys, axis=0)))
        return jnp.where(causal, jnp.concatenate(att, axis=0), 0.0).astype(BF16)

    atts = [gla_scores(h) for h in range(GLA_HEADS)]
    gn = gn_ref[...]
    for h in range(GLA_HEADS):
        ks = slice(h * GLA_DK, (h + 1) * GLA_DK)
        vs = slice(h * GLA_DV, (h + 1) * GLA_DV)
        st = st_ref[h]
        o = _dot(atts[h], va[:, vs]) + _dot(q_in[:, ks], st.astype(BF16))
        dec_col = jnp.transpose(jnp.broadcast_to(dec_tile[:, ks], (SUBLANES, GLA_DK)))[:, 0:1]
        st_ref[h] = st * dec_col + _dot_tn(k_tile[:, ks], va[:, vs])
        o = o * lax.rsqrt(jnp.mean(o * o, axis=-1, keepdims=True) + EPS) * gn
        oa_ref[:, vs] = o * _silu(ra[:, vs])

    wd = WINDOW
    k_all = jnp.concatenate([kprev_ref[...], kb], axis=0)
    v_all = jnp.concatenate([vprev_ref[...], vb], axis=0)
    kprev_ref[...] = kb[ts - wd:, :]
    vprev_ref[...] = vb[ts - wd:, :]
    lo_half = lax.broadcasted_iota(jnp.int32, (wd + ts, LANES), 1) < SWA_DH
    k_sw = pltpu.roll(k_all, SWA_DH, 1)
    v_sw = pltpu.roll(v_all, SWA_DH, 1)
    k2 = (jnp.where(lo_half, k_all, k_sw), jnp.where(lo_half, k_sw, k_all))
    v2 = (jnp.where(lo_half, v_all, v_sw), jnp.where(lo_half, v_sw, v_all))
    qi = lax.broadcasted_iota(jnp.int32, (wd, 2 * wd), 0) + wd
    kj = lax.broadcasted_iota(jnp.int32, (wd, 2 * wd), 1)
    band = (qi - kj >= 0) & (qi - kj < wd)
    band_first = band & (jnp.logical_not(first) | (kj >= wd)) if first is not False else band
    q_lo = lax.broadcasted_iota(jnp.int32, (wd, LANES), 1) < SWA_DH
    row0 = lax.broadcasted_iota(jnp.int32, (2 * wd, LANES), 0) == 0
    ones = jnp.ones((2 * wd, LANES), BF16)
    half = SWA_GROUP // 2
    combos = [(blk, hk) for blk in range(ts // wd) for hk in range(SWA_KV_HEADS)]

    def swa_probs(blk, hk):
        parts = []
        for p in range(half):
            qp = qb[blk * wd:(blk + 1) * wd, (hk * half + p) * LANES:(hk * half + p + 1) * LANES]
            parts += [jnp.where(q_lo, qp, jnp.zeros_like(qp)), jnp.where(q_lo, jnp.zeros_like(qp), qp)]
        s = _dot_nt(jnp.concatenate(parts, axis=0), k2[hk][blk * wd:blk * wd + 2 * wd])
        valid = band_first if blk == 0 else band
        es = []
        for r in range(SWA_GROUP):
            head = hk * SWA_GROUP + r
            sh = jnp.where(valid, s[r * wd:(r + 1) * wd], fill_ref[head:head + 1, :])
            es.append(jnp.exp(sh - jnp.max(sh, axis=-1, keepdims=True)).astype(BF16))
        return jnp.concatenate(es, axis=0)

    probs = [swa_probs(blk, hk) for blk, hk in combos]
    side_out = advance()
    ga = _dot(hb, w[GA][...])
    gb = _dot(hb, w[GB][...])
    for (blk, hk), e in zip(combos, probs):
        vv = jnp.where(row0, jnp.zeros_like(ones), v2[hk][blk * wd:blk * wd + 2 * wd])
        o = _dot(e, jnp.concatenate([vv, ones], axis=1))
        for p in range(half):
            oe = o[(2 * p) * wd:(2 * p + 1) * wd]
            oo = o[(2 * p + 1) * wd:(2 * p + 2) * wd]
            num = jnp.where(q_lo, oe[:, :LANES], oo[:, :LANES])
            den = jnp.where(q_lo, oe[:, LANES:], oo[:, LANES:])
            ob_ref[blk * wd:(blk + 1) * wd, (hk * half + p) * LANES:(hk * half + p + 1) * LANES] = num / den

    merged = _sigmoid(ga) * oa_ref[...] + _sigmoid(gb) * ob_ref[...]
    return x + gt_ref[...] * _dot(merged.astype(BF16), w[WO][...]), side_out


def _route_and_sort(hb, wrt_ref, brt_ref, upper_ref):
    tm = hb.shape[0]
    neg = -jnp.inf
    lt = _dot_nt(wrt_ref[...], hb) + brt_ref[...]
    yield None
    lrow = lambda k: lt[ROUTE_STRIDE * k:ROUTE_STRIDE * k + 1, :]
    lg = [lrow(g) for g in range(N_GROUPS)]
    gmax = functools.reduce(jnp.maximum, lg)
    g_w = 1.0 / functools.reduce(jnp.add, [jnp.exp(v - gmax) for v in lg])
    g_idx = _first_of(lg, gmax)
    le = []
    for j in range(EXPERTS_PER_GROUP):
        v = lrow(N_GROUPS + (N_GROUPS - 1) * EXPERTS_PER_GROUP + j)
        for g in range(N_GROUPS - 2, -1, -1):
            v = jnp.where(g_idx == g, lrow(N_GROUPS + g * EXPERTS_PER_GROUP + j), v)
        le.append(v)
    v1 = functools.reduce(jnp.maximum, le)
    i1 = _first_of(le, v1)
    le2 = [jnp.where(i1 == j, neg, le[j]) for j in range(EXPERTS_PER_GROUP)]
    v2 = functools.reduce(jnp.maximum, le2)
    i2 = _first_of(le2, v2)
    e2 = jnp.exp(v2 - v1)
    w1 = g_w / (1.0 + e2)
    w2 = g_w * e2 / (1.0 + e2)

    sub = lax.broadcasted_iota(jnp.int32, (ROW_PACK, tm), 0)
    rank = _dot((sub == g_idx).astype(BF16), upper_ref[...])
    yield None
    pcs, offs, run = [], [], 0
    pos = jnp.zeros((1, tm), F32)
    for g in range(N_GROUPS):
        mine = g_idx == g
        cnt = jnp.sum(mine.astype(F32), axis=1, keepdims=True)[0, 0].astype(jnp.int32)
        pcs.append(((cnt + (ROW_PACK - 1)) // ROW_PACK) * ROW_PACK)
        offs.append(run)
        pos = pos + jnp.where(mine, rank[g:g + 1, :] + jnp.asarray(run, F32), 0.0)
        run = run + pcs[g]
    perm = (lax.broadcasted_iota(jnp.int32, (SORT_ROWS, tm), 0) == pos.astype(jnp.int32)).astype(BF16)
    cw = jnp.where(sub == i1, w1, 0.0) + jnp.where(sub == i2, w2, 0.0)
    cw_hi = cw.astype(BF16).astype(F32)
    packed = cw_hi + pltpu.roll(cw - cw_hi, EXPERTS_PER_GROUP, 0)
    cw_cols = _rows_to_cols(jnp.concatenate([packed, jnp.zeros((LANES - ROW_PACK, tm), F32)], axis=0))
    sorted_rows = _dot(perm, jnp.concatenate([hb, cw_cols.astype(BF16)], axis=1)).astype(BF16)
    yield sorted_rows, pos, pcs, offs


def _mixer_call(x2, vecs1, vecs2, w2p, bgk, gn, fill, wrt, brt, weights, experts_f32, cap, bsz, seq):
    g1, sc1, sh1, gt1 = vecs1
    g2, sc2, sh2 = vecs2
    ns = seq // TM_MIX
    n_steps = bsz * ns
    tok = pl.BlockSpec((TM_MIX, D_MODEL), lambda b, s: (b * ns + s, 0))
    vec = pl.BlockSpec((None, 1, D_MODEL), lambda b, s: (b, 0, 0))
    full = lambda a: pl.BlockSpec(a.shape, lambda b, s: (0,) * a.ndim)
    hbm = pl.BlockSpec(memory_space=pl.ANY)
    slab = lambda a: pl.BlockSpec((a.shape[0] // n_steps, a.shape[1]), lambda b, s: (b * ns + s, 0))
    idx = jnp.arange(MIX_SUB)
    tri = ((idx[:, None] // GLA_CHUNK == idx[None, :] // GLA_CHUNK) & (idx[None, :] <= idx[:, None])).astype(BF16)
    upper = (idx[:, None] < idx[None, :]).astype(BF16)
    ex2d = [e.reshape(-1, e.shape[-1]) for e in experts_f32]
    n_sub = n_steps * SUBS
    vmem_in = [x2, g1, sc1, sh1, gt1, w2p, bgk, tri, gn, fill, g2, sc2, sh2, wrt, brt, upper]
    outs = pl.pallas_call(
        _mixer_kernel,
        grid=(bsz, ns),
        in_specs=[tok, full(g1), vec, vec, vec, full(w2p), full(bgk), full(tri), full(gn), full(fill),
                  full(g2), vec, vec, full(wrt), full(brt), full(upper)]
        + [slab(e) for e in ex2d] + [hbm] * N_WEIGHTS,
        out_specs=[tok, hbm, hbm, pl.BlockSpec(memory_space=pltpu.SMEM)] + [slab(e) for e in ex2d],
        out_shape=[jax.ShapeDtypeStruct(x2.shape, F32),
                   jax.ShapeDtypeStruct((N_GROUPS, cap, HS_W), BF16),
                   jax.ShapeDtypeStruct((n_sub, SUBLANES, MIX_SUB), F32),
                   jax.ShapeDtypeStruct((n_sub * META,), jnp.int32)]
        + [jax.ShapeDtypeStruct(e.shape, BF16) for e in ex2d],
        scratch_shapes=[pltpu.VMEM(wt.shape, BF16) for wt in weights]
        + [pltpu.VMEM((GLA_HEADS, GLA_DK, GLA_DV), F32),
           pltpu.VMEM((WINDOW, SWA_KV_W), BF16), pltpu.VMEM((WINDOW, SWA_KV_W), BF16),
           pltpu.VMEM((MIX_SUB, GLA_VAL_W), F32), pltpu.VMEM((MIX_SUB, SWA_W), F32),
           pltpu.VMEM((SUBS, SORT_ROWS, HS_W), BF16), pltpu.VMEM((SUBS, SUBLANES, MIX_SUB), F32),
           pltpu.VMEM((MIX_SUB, D_MODEL), BF16), pltpu.VMEM((TM_EXP, HS_W), BF16),
           pltpu.SMEM((N_GROUPS,), jnp.int32), pltpu.SemaphoreType.DMA((SUBS,))],
        compiler_params=_params(("arbitrary", "arbitrary")),
        name="mixer",
    )(*vmem_in, *ex2d, *weights)
    x1, hs, pos, meta = outs[:4]
    experts_bf16 = [o.reshape(e.shape) for o, e in zip(outs[4:], experts_f32)]
    return x1, hs, pos, meta, experts_bf16


def _experts_kernel(grp_ref, blk_ref, valid_ref, hs_ref, wg_ref, wu_ref, wd_ref, ys_ref):
    @pl.when(valid_ref[pl.program_id(0)] != 0)
    def _():
        hx = hs_ref[...]
        h = hx[:, :D_MODEL]
        cw = hx[:, D_MODEL:].astype(F32)
        gates = [_dot(h, wg_ref[j]) for j in range(EXPERTS_PER_GROUP)]
        ups = [_dot(h, wu_ref[j]) for j in range(EXPERTS_PER_GROUP)]
        hids = [(_silu(gates[j]) * ups[j]
                 * (cw[:, j:j + 1] + cw[:, EXPERTS_PER_GROUP + j:EXPERTS_PER_GROUP + j + 1])).astype(BF16)
                for j in range(EXPERTS_PER_GROUP)]
        y = _dot(hids[0], wd_ref[0])
        for j in range(1, EXPERTS_PER_GROUP):
            y = y + _dot(hids[j], wd_ref[j])
        ys_ref[...] = y.astype(BF16)


def _experts_call(grp, blk, valid, hs, wg, wu, wd):
    n_work = grp.shape[0]
    rows = lambda width: pl.BlockSpec((None, TM_EXP, width), lambda i, grp, blk, valid: (grp[i], blk[i], 0))
    wspec = lambda a: pl.BlockSpec((EXPERTS_PER_GROUP,) + a.shape[1:], lambda i, grp, blk, valid: (grp[i], 0, 0))
    return pl.pallas_call(
        _experts_kernel,
        grid_spec=pltpu.PrefetchScalarGridSpec(
            num_scalar_prefetch=3,
            grid=(n_work,),
            in_specs=[rows(HS_W), wspec(wg), wspec(wu), wspec(wd)],
            out_specs=rows(D_MODEL)),
        out_shape=jax.ShapeDtypeStruct(hs.shape[:2] + (D_MODEL,), BF16),
        compiler_params=_params(("arbitrary",)),
        name="experts",
    )(grp, blk, valid, hs, wg, wu, wd)


def _final_kernel(meta_ref, x1_ref, pos_ref, gt_ref, gf_ref, ys_hbm, o_ref, ybuf_ref, sem):
    step = pl.program_id(0)

    def fetch(s, action):
        half = s % 2
        for sub in range(FIN_SUBS):
            m0 = (s * FIN_SUBS + sub) * META
            bases = [meta_ref[m0 + g] for g in range(N_GROUPS)]
            pcs = [meta_ref[m0 + N_GROUPS + g] for g in range(N_GROUPS)]
            offs, run = [], 0
            for g in range(N_GROUPS):
                offs.append(run)
                run = run + pcs[g]
            _segment_copies(pcs, bases, offs, lambda g, src_row, dst_row, size, sub=sub: pltpu.make_async_copy(
                ys_hbm.at[g, pl.ds(src_row, size), :], ybuf_ref.at[half, sub, pl.ds(dst_row, size), :],
                sem.at[half]), action)

    @pl.when(step == 0)
    def _():
        ybuf_ref[...] = jnp.zeros_like(ybuf_ref)
        fetch(step, "start")

    @pl.when(step + 1 < pl.num_programs(0))
    def _():
        fetch(step + 1, "start")

    fetch(step, "wait")
    half = step % 2
    for sub in range(FIN_SUBS):
        r0 = sub * MIX_SUB
        pos_col = _rows_to_cols(pos_ref[sub])[:, 0:1].astype(jnp.int32)
        sel = (lax.broadcasted_iota(jnp.int32, (MIX_SUB, SORT_ROWS), 1) == pos_col).astype(BF16)
        y = _dot(sel, ybuf_ref[half, sub])
        o_ref[r0:r0 + MIX_SUB, :] = _rms(x1_ref[r0:r0 + MIX_SUB, :] + gt_ref[...] * y, gf_ref[...])


def _final_call(meta, x1, pos, gt2, gf, ys, seq):
    t = x1.shape[0]
    per_b = seq // TM_FIN
    tok = pl.BlockSpec((TM_FIN, D_MODEL), lambda i, meta: (i, 0))
    return pl.pallas_call(
        _final_kernel,
        grid_spec=pltpu.PrefetchScalarGridSpec(
            num_scalar_prefetch=1,
            grid=(t // TM_FIN,),
            in_specs=[tok, pl.BlockSpec((FIN_SUBS, SUBLANES, MIX_SUB), lambda i, meta: (i, 0, 0)),
                      pl.BlockSpec((None, 1, D_MODEL), lambda i, meta: (i // per_b, 0, 0)),
                      pl.BlockSpec((1, D_MODEL), lambda i, meta: (0, 0)),
                      pl.BlockSpec(memory_space=pl.ANY)],
            out_specs=tok,
            scratch_shapes=[pltpu.VMEM((2, FIN_SUBS, SORT_ROWS, D_MODEL), BF16), pltpu.SemaphoreType.DMA((2,))]),
        out_shape=jax.ShapeDtypeStruct((t, D_MODEL), F32),
        compiler_params=_params(("arbitrary",)),
        name="final",
    )(meta, x1, pos, gt2, gf, ys)


def kernel(x, c, w_ada, b_ada, norm1_g, w_in, w_gk2, b_gk, gla_norm_g, sink, w_o, norm2_g,
           w_group, b_group, w_router, b_router, w_gate, w_up, w_down, norm_f_g):
    bsz, seq, d = x.shape
    assert w_ada.shape[0] == 1, "single layer: the final norm is fused behind the layer's MoE"
    l = 0
    t = bsz * seq
    n_sub = t // MIX_SUB
    cap = -(-(t + n_sub * (ROW_PACK - 1) + TM_EXP) // TM_EXP) * TM_EXP
    n_work = t // TM_EXP + n_sub * (ROW_PACK - 1) * N_GROUPS // TM_EXP + N_GROUPS + 1

    mod = _mod_call(c, w_ada[l], b_ada[l])
    sh1, sc1, gt1, sh2, sc2, gt2 = [m.reshape(bsz, 1, d) for m in jnp.split(mod, N_MOD, axis=-1)]

    w_parts = _split_cols(w_in[l])
    w_parts[GATE] = jnp.pad(w_parts[GATE], ((0, 0), (0, LANES - GLA_GATE_RANK)))
    weights = [wt.astype(BF16) for wt in w_parts + [w_o[l]]]
    w2p = jnp.pad(w_gk2[l], ((0, LANES - GLA_GATE_RANK), (0, 0))).astype(BF16)
    fill = jnp.full((SWA_HEADS, 2 * WINDOW), -jnp.inf, F32).at[:, 0].set(sink[l])
    n_logit = N_GROUPS + N_EXPERTS
    wrt = jnp.zeros((n_logit, ROUTE_STRIDE, d), F32).at[:, 0, :].set(
        jnp.concatenate([w_group[l], w_router[l]], axis=1).T).reshape(n_logit * ROUTE_STRIDE, d).astype(BF16)
    brt = jnp.zeros((n_logit, ROUTE_STRIDE), F32).at[:, 0].set(
        jnp.concatenate([b_group[l], b_router[l]])).reshape(n_logit * ROUTE_STRIDE, 1)

    x1, hs, pos, meta, (wg, wu, wd) = _mixer_call(
        x.reshape(t, d), (norm1_g[l].reshape(1, d), sc1, sh1, gt1), (norm2_g[l].reshape(1, d), sc2, sh2),
        w2p, b_gk[l].reshape(1, GLA_KEY_W), gla_norm_g[l].reshape(1, GLA_DV), fill, wrt, brt, weights,
        (w_gate[l], w_up[l], w_down[l]), cap, bsz, seq)

    last = meta.reshape(n_sub, META)[-1]
    totals = last[:N_GROUPS] + last[N_GROUPS:]
    ends = jnp.cumsum((totals + TM_EXP - 1) // TM_EXP)
    item = jnp.minimum(jnp.arange(n_work, dtype=jnp.int32), ends[-1] - 1)
    grp = jnp.sum(item[:, None] >= ends[None, :], axis=1).astype(jnp.int32)
    blk = item - jnp.concatenate([jnp.zeros((1,), ends.dtype), ends[:-1]])[grp]
    valid = (jnp.arange(n_work) < ends[-1]).astype(jnp.int32)

    ys = _experts_call(grp, blk.astype(jnp.int32), valid, hs, wg, wu, wd)
    out = _final_call(meta, x1, pos, gt2, norm_f_g.reshape(1, d), ys, seq)
    return out.reshape(bsz, seq, d)


def _split_cols(w):
    parts, start = [], 0
    for width in IN_WIDTHS:
        parts.append(w[:, start:start + width])
        start += width
    return parts
```

```python
import functools

import jax
import jax.numpy as jnp
from jax import lax
from jax.experimental import pallas as pl
from jax.experimental.pallas import tpu as pltpu

F32 = jnp.float32
BF16 = jnp.bfloat16

D_MODEL = 1024
GLA_HEADS = 4
GLA_DK = 128
GLA_DV = 256
GLA_KEY_W = GLA_HEADS * GLA_DK
GLA_VAL_W = GLA_HEADS * GLA_DV
GLA_GATE_RANK = 16
GLA_TAU = 16.0
GLA_CHUNK = 64
SWA_HEADS = 16
SWA_KV_HEADS = 2
SWA_GROUP = SWA_HEADS // SWA_KV_HEADS
SWA_DH = 64
SWA_W = SWA_HEADS * SWA_DH
SWA_KV_W = SWA_KV_HEADS * SWA_DH
WINDOW = 128
N_GROUPS = 4
EXPERTS_PER_GROUP = 4
N_EXPERTS = N_GROUPS * EXPERTS_PER_GROUP
D_EXPERT = 256
N_MOD = 6
EPS = 1e-6
IN_WIDTHS = (GLA_KEY_W, GLA_KEY_W, GLA_VAL_W, GLA_VAL_W, GLA_GATE_RANK,
             SWA_W, SWA_KV_W, SWA_KV_W, D_MODEL, D_MODEL)

LANES = 128
LOG2_E = 1.4426950408889634
LN_2 = 0.6931471805599453
VMEM_LIMIT = 56 * 1024 * 1024

TM_MIX = 512
MIX_SUB = 256
SUBS = TM_MIX // MIX_SUB
TM_FIN = 1024
FIN_SUBS = TM_FIN // MIX_SUB
ROW_PACK = 16
SORT_ROWS = MIX_SUB + 64
SEG_BITS = (256, 128, 64, 32, 16)
HS_W = D_MODEL + LANES
TM_EXP = 1024
META = 2 * N_GROUPS
SUBLANES = 8
ROUTE_STRIDE = SUBLANES
TN_MOD = 512


def _params(sem):
    return pltpu.CompilerParams(dimension_semantics=sem, vmem_limit_bytes=VMEM_LIMIT)


def _split(a):
    hi = a.astype(BF16)
    lo = (a - hi.astype(F32)).astype(BF16)
    return hi, lo


def _dot(a, b):
    return jnp.dot(a, b, preferred_element_type=F32)


def _dot_nt(a, b):
    return lax.dot_general(a, b, (((1,), (1,)), ((), ())), preferred_element_type=F32)


def _dot_tn(a, b):
    return lax.dot_general(a, b, (((0,), (0,)), ((), ())), preferred_element_type=F32)


def _dot3(a, b):
    a_hi, a_lo = _split(a)
    b_hi, b_lo = _split(b)
    return _dot(a_hi, b_hi) + _dot(a_hi, b_lo) + _dot(a_lo, b_hi)


def _sigmoid(x):
    return 1.0 / (1.0 + jnp.exp(-x))


def _silu(x):
    return x * _sigmoid(x)


def _rms(x, g):
    return x * lax.rsqrt(jnp.mean(x * x, axis=-1, keepdims=True) + EPS) * g


def _first_of(vals, target):
    idx = jnp.full(target.shape, len(vals) - 1, jnp.int32)
    for j in range(len(vals) - 2, -1, -1):
        idx = jnp.where(vals[j] == target, j, idx)
    return idx


def _rows_to_cols(a):
    return jnp.concatenate([jnp.transpose(a[:, i * LANES:(i + 1) * LANES]) for i in range(a.shape[1] // LANES)],
                           axis=0)


def _segment_copies(pcs, src_rows, dst_rows, make_copy, action):
    for g in range(N_GROUPS):
        for bit in SEG_BITS:
            done = pcs[g] & ~(2 * bit - 1)

            @pl.when((pcs[g] & bit) != 0)
            def _(g=g, bit=bit, done=done):
                cp = make_copy(g, pl.multiple_of(src_rows[g] + done, ROW_PACK),
                               pl.multiple_of(dst_rows[g] + done, ROW_PACK), bit)
                cp.start() if action == "start" else cp.wait()


def _mod_kernel(c_ref, w_ref, b_ref, o_ref):
    o_ref[...] = _dot3(_silu(c_ref[...]), w_ref[...]) + b_ref[...]


def _mod_call(c, w_ada, b_ada):
    bsz = c.shape[0]
    n = w_ada.shape[1]
    return pl.pallas_call(
        _mod_kernel,
        grid=(n // TN_MOD,),
        in_specs=[pl.BlockSpec((bsz, D_MODEL), lambda j: (0, 0)),
                  pl.BlockSpec((D_MODEL, TN_MOD), lambda j: (0, j)),
                  pl.BlockSpec((1, TN_MOD), lambda j: (0, j))],
        out_specs=pl.BlockSpec((bsz, TN_MOD), lambda j: (0, j)),
        out_shape=jax.ShapeDtypeStruct((bsz, n), F32),
        compiler_params=_params(("parallel",)),
        name="mod",
    )(c, w_ada, b_ada.reshape(1, n))


QA, KA, VA, RA, GATE, QB, KB, VB, GA, GB, WO = range(11)
N_WEIGHTS = 11


def _mixer_kernel(x_ref, g1_ref, sc1_ref, sh1_ref, gt1_ref, w2_ref, bgk_ref, tri_ref, gn_ref, fill_ref,
                  g2_ref, sc2_ref, sh2_ref, wrt_ref, brt_ref, upper_ref, wgf_ref, wuf_ref, wdf_ref, *rest):
    w_hbm = rest[:N_WEIGHTS]
    x1_ref, hs_hbm, pos_hbm, meta_ref, wgb_ref, wub_ref, wdb_ref = rest[N_WEIGHTS:N_WEIGHTS + 7]
    w = rest[N_WEIGHTS + 7:2 * N_WEIGHTS + 7]
    (st_ref, kprev_ref, vprev_ref, oa_ref, ob_ref, stage_ref, pstage_ref, hkeep_ref, zero_ref, base_ref,
     sem) = rest[2 * N_WEIGHTS + 7:]
    first = pl.program_id(1) == 0
    step = pl.program_id(0) * pl.num_programs(1) + pl.program_id(1)
    n_steps = pl.num_programs(0) * pl.num_programs(1)

    @pl.when(step == 0)
    def _():
        loads = [pltpu.make_async_copy(src, dst, sem.at[0]) for src, dst in zip(w_hbm, w)]
        for cp in loads:
            cp.start()
        for cp in loads:
            cp.wait()
        for g in range(N_GROUPS):
            base_ref[g] = 0
        hkeep_ref[...] = jnp.zeros_like(hkeep_ref)

    @pl.when(first)
    def _():
        st_ref[...] = jnp.zeros_like(st_ref)
        kprev_ref[...] = jnp.zeros_like(kprev_ref)
        vprev_ref[...] = jnp.zeros_like(vprev_ref)

    wgb_ref[...] = wgf_ref[...].astype(BF16)
    wub_ref[...] = wuf_ref[...].astype(BF16)
    wdb_ref[...] = wdf_ref[...].astype(BF16)

    def stage_copy(slot):
        return lambda g, src_row, dst_row, size: pltpu.make_async_copy(
            stage_ref.at[slot, pl.ds(src_row, size), :], hs_hbm.at[g, pl.ds(dst_row, size), :], sem.at[slot])

    pos_copy = lambda slot, tile: pltpu.make_async_copy(pstage_ref.at[slot], pos_hbm.at[tile], sem.at[slot])
    zeros4 = [0] * N_GROUPS

    def wait_copies(sub, tile, pcs):
        _segment_copies(pcs, zeros4, zeros4, stage_copy(sub), "wait")
        pos_copy(sub, tile).wait()

    def publish(sub, owner, routed):
        sorted_rows, pos, pcs, offs = routed
        tile = owner * SUBS + sub
        m0 = tile * META

        @pl.when(owner > 0)
        def _():
            wait_copies(sub, tile, [meta_ref[m0 - SUBS * META + N_GROUPS + g] for g in range(N_GROUPS)])

        stage_ref[sub] = sorted_rows
        pstage_ref[sub] = jnp.broadcast_to(pos, (SUBLANES, MIX_SUB))
        bases = [base_ref[g] for g in range(N_GROUPS)]
        for g in range(N_GROUPS):
            meta_ref[m0 + g] = bases[g]
            meta_ref[m0 + N_GROUPS + g] = pcs[g]
            base_ref[g] = bases[g] + pcs[g]
        _segment_copies(pcs, offs, bases, stage_copy(sub), "start")
        pos_copy(sub, tile).start()

    route = lambda hb: _route_and_sort(hb, wrt_ref, brt_ref, upper_ref)
    moe_in = lambda x1: (_rms(x1, g2_ref[...]) * (1.0 + sc2_ref[...]) + sh2_ref[...]).astype(BF16)
    last_step = step == n_steps - 1

    routing, results = route(hkeep_ref[...]), []
    for sub in range(SUBS):
        r0 = sub * MIX_SUB
        x1, routed = _mixer_tile(x_ref[r0:r0 + MIX_SUB, :], first if sub == 0 else False, g1_ref, sc1_ref,
                                 sh1_ref, gt1_ref, w2_ref, bgk_ref, tri_ref, gn_ref, fill_ref, w, st_ref,
                                 kprev_ref, vprev_ref, oa_ref, ob_ref, side=routing)
        x1_ref[r0:r0 + MIX_SUB, :] = x1
        results.append(routed)
        if sub + 1 < SUBS:
            routing = route(moe_in(x1))
        else:
            hkeep_ref[...] = moe_in(x1)

    @pl.when(step > 0)
    def _():
        publish(SUBS - 1, step - 1, results[0])

    for sub in range(1, SUBS):
        publish(sub - 1, step, results[sub])

    @pl.when(last_step)
    def _():
        routing = route(hkeep_ref[...])
        next(routing)
        next(routing)
        routed = next(routing)
        publish(SUBS - 1, step, routed)
        for sub in range(1, SUBS):
            wait_copies(sub - 1, step * SUBS + sub - 1, results[sub][2])
        wait_copies(SUBS - 1, step * SUBS + SUBS - 1, routed[2])
        zero_ref[...] = jnp.zeros_like(zero_ref)
        tails = [pltpu.make_async_copy(zero_ref, hs_hbm.at[g, pl.ds(pl.multiple_of(base_ref[g], ROW_PACK), TM_EXP), :],
                                       sem.at[0]) for g in range(N_GROUPS)]
        for cp in tails:
            cp.start()
        for cp in tails:
            cp.wait()


def _mixer_tile(x, first, g1_ref, sc_ref, sh_ref, gt_ref, w2_ref, bgk_ref, tri_ref, gn_ref, fill_ref,
                w, st_ref, kprev_ref, vprev_ref, oa_ref, ob_ref, side=None):
    ts = MIX_SUB
    advance = (lambda: next(side)) if side is not None else (lambda: None)
    hb = (_rms(x, g1_ref[...]) * (1.0 + sc_ref[...]) + sh_ref[...]).astype(BF16)

    z = _dot(hb, w[GATE][...])
    gz = _dot(z.astype(BF16), w2_ref[...]) + bgk_ref[...]
    soft = jnp.log2(1.0 + jnp.exp2(jnp.abs(gz) * (-LOG2_E)))
    la_hi, la_lo = _split(jnp.minimum(gz, 0.0) * (1.0 / GLA_TAU) - soft * (LN_2 / GLA_TAU))
    qa = _dot(hb, w[QA][...])
    ka = _dot(hb, w[KA][...])
    advance()
    va = _dot(hb, w[VA][...]).astype(BF16)
    bc = _dot(tri_ref[...], la_hi) + _dot(tri_ref[...], la_lo)
    ra = _dot(hb, w[RA][...])
    advance()
    qb = _dot(hb, w[QB][...]).astype(BF16) * jnp.asarray(SWA_DH ** -0.5, BF16)
    kb = _dot(hb, w[KB][...]).astype(BF16)
    vb = _dot(hb, w[VB][...]).astype(BF16)

    nc = ts // GLA_CHUNK
    chunk = lambda c: slice(c * GLA_CHUNK, (c + 1) * GLA_CHUNK)
    btot = [bc[(c + 1) * GLA_CHUNK - 1:(c + 1) * GLA_CHUNK, :] for c in range(nc)]
    pre = [jnp.zeros_like(btot[0])]
    for c in range(nc):
        pre.append(pre[c] + btot[c])
    rows = lambda vecs: jnp.concatenate([jnp.broadcast_to(v, (GLA_CHUNK, GLA_KEY_W)) for v in vecs], axis=0)
    qf = qa * (GLA_DK ** -0.5) * jnp.exp(bc)
    q_dec = qf.astype(BF16)
    q_in = (qf * rows([jnp.exp(pre[c]) for c in range(nc)])).astype(BF16)
    k_inv = (ka * jnp.exp(-bc)).astype(BF16)
    k_end = ka * jnp.exp(rows(btot) - bc)
    k_tile = (k_end * rows([jnp.exp(pre[nc] - pre[c + 1]) for c in range(nc)])).astype(BF16)
    k_cross = {(c, c2): (k_end[chunk(c2)] * jnp.exp(pre[c] - pre[c2 + 1])).astype(BF16)
               for c in range(nc) for c2 in range(c)}
    dec_tile = jnp.exp(pre[nc])
    causal = (lax.broadcasted_iota(jnp.int32, (ts, ts), 1) <= lax.broadcasted_iota(jnp.int32, (ts, ts), 0))

    def gla_scores(h):
        ks = slice(h * GLA_DK, (h + 1) * GLA_DK)
        att = []
        for c in range(nc):
            keys = [k_cross[(c, c2)][:, ks] for c2 in range(c)] + [k_inv[chunk(c), ks]]
            keys += [jnp.zeros((GLA_CHUNK, GLA_DK), BF16)] * (nc - 1 - c)
            att.append(_dot_nt(q_dec[chunk(c), ks], jnp.concatenate(keys, axis=0)))
        return jnp.where(causal, jnp.concatenate(att, axis=0), 0.0).astype(BF16)

    atts = [gla_scores(h) for h in range(GLA_HEADS)]
    gn = gn_ref[...]
    for h in range(GLA_HEADS):
        ks = slice(h * GLA_DK, (h + 1) * GLA_DK)
        vs = slice(h * GLA_DV, (h + 1) * GLA_DV)
        st = st_ref[h]
        o = _dot(atts[h], va[:, vs]) + _dot(q_in[:, ks], st.astype(BF16))
        dec_col = jnp.transpose(jnp.broadcast_to(dec_tile[:, ks], (SUBLANES, GLA_DK)))[:, 0:1]
        st_ref[h] = st * dec_col + _dot_tn(k_tile[:, ks], va[:, vs])
        o = o * lax.rsqrt(jnp.mean(o * o, axis=-1, keepdims=True) + EPS) * gn
        oa_ref[:, vs] = o * _silu(ra[:, vs])

    wd = WINDOW
    k_all = jnp.concatenate([kprev_ref[...], kb], axis=0)
    v_all = jnp.concatenate([vprev_ref[...], vb], axis=0)
    kprev_ref[...] = kb[ts - wd:, :]
    vprev_ref[...] = vb[ts - wd:, :]
    lo_half = lax.broadcasted_iota(jnp.int32, (wd + ts, LANES), 1) < SWA_DH
    k_sw = pltpu.roll(k_all, SWA_DH, 1)
    v_sw = pltpu.roll(v_all, SWA_DH, 1)
    k2 = (jnp.where(lo_half, k_all, k_sw), jnp.where(lo_half, k_sw, k_all))
    v2 = (jnp.where(lo_half, v_all, v_sw), jnp.where(lo_half, v_sw, v_all))
    qi = lax.broadcasted_iota(jnp.int32, (wd, 2 * wd), 0) + wd
    kj = lax.broadcasted_iota(jnp.int32, (wd, 2 * wd), 1)
    band = (qi - kj >= 0) & (qi - kj < wd)
    band_first = band & (jnp.logical_not(first) | (kj >= wd)) if first is not False else band
    q_lo = lax.broadcasted_iota(jnp.int32, (wd, LANES), 1) < SWA_DH
    row0 = lax.broadcasted_iota(jnp.int32, (2 * wd, LANES), 0) == 0
    ones = jnp.ones((2 * wd, LANES), BF16)
    half = SWA_GROUP // 2
    combos = [(blk, hk) for blk in range(ts // wd) for hk in range(SWA_KV_HEADS)]

    def swa_probs(blk, hk):
        parts = []
        for p in range(half):
            qp = qb[blk * wd:(blk + 1) * wd, (hk * half + p) * LANES:(hk * half + p + 1) * LANES]
            parts += [jnp.where(q_lo, qp, jnp.zeros_like(qp)), jnp.where(q_lo, jnp.zeros_like(qp), qp)]
        s = _dot_nt(jnp.concatenate(parts, axis=0), k2[hk][blk * wd:blk * wd + 2 * wd])
        valid = band_first if blk == 0 else band
        es = []
        for r in range(SWA_GROUP):
            head = hk * SWA_GROUP + r
            sh = jnp.where(valid, s[r * wd:(r + 1) * wd], fill_ref[head:head + 1, :])
            es.append(jnp.exp(sh - jnp.max(sh, axis=-1, keepdims=True)).astype(BF16))
        return jnp.concatenate(es, axis=0)

    probs = [swa_probs(blk, hk) for blk, hk in combos]
    side_out = advance()
    ga = _dot(hb, w[GA][...])
    gb = _dot(hb, w[GB][...])
    for (blk, hk), e in zip(combos, probs):
        vv = jnp.where(row0, jnp.zeros_like(ones), v2[hk][blk * wd:blk * wd + 2 * wd])
        o = _dot(e, jnp.concatenate([vv, ones], axis=1))
        for p in range(half):
            oe = o[(2 * p) * wd:(2 * p + 1) * wd]
            oo = o[(2 * p + 1) * wd:(2 * p + 2) * wd]
            num = jnp.where(q_lo, oe[:, :LANES], oo[:, :LANES])
            den = jnp.where(q_lo, oe[:, LANES:], oo[:, LANES:])
            ob_ref[blk * wd:(blk + 1) * wd, (hk * half + p) * LANES:(hk * half + p + 1) * LANES] = num / den

    merged = _sigmoid(ga) * oa_ref[...] + _sigmoid(gb) * ob_ref[...]
    return x + gt_ref[...] * _dot(merged.astype(BF16), w[WO][...]), side_out


def _route_and_sort(hb, wrt_ref, brt_ref, upper_ref):
    tm = hb.shape[0]
    neg = -jnp.inf
    lt = _dot_nt(wrt_ref[...], hb) + brt_ref[...]
    yield None
    lrow = lambda k: lt[ROUTE_STRIDE * k:ROUTE_STRIDE * k + 1, :]
    lg = [lrow(g) for g in range(N_GROUPS)]
    gmax = functools.reduce(jnp.maximum, lg)
    g_w = 1.0 / functools.reduce(jnp.add, [jnp.exp(v - gmax) for v in lg])
    g_idx = _first_of(lg, gmax)
    le = []
    for j in range(EXPERTS_PER_GROUP):
        v = lrow(N_GROUPS + (N_GROUPS - 1) * EXPERTS_PER_GROUP + j)
        for g in range(N_GROUPS - 2, -1, -1):
            v = jnp.where(g_idx == g, lrow(N_GROUPS + g * EXPERTS_PER_GROUP + j), v)
        le.append(v)
    v1 = functools.reduce(jnp.maximum, le)
    i1 = _first_of(le, v1)
    le2 = [jnp.where(i1 == j, neg, le[j]) for j in range(EXPERTS_PER_GROUP)]
    v2 = functools.reduce(jnp.maximum, le2)
    i2 = _first_of(le2, v2)
    e2 = jnp.exp(v2 - v1)
    w1 = g_w / (1.0 + e2)
    w2 = g_w * e2 / (1.0 + e2)

    sub = lax.broadcasted_iota(jnp.int32, (ROW_PACK, tm), 0)
    rank = _dot((sub == g_idx).astype(BF16), upper_ref[...])
    yield None
    pcs, offs, run = [], [], 0
    pos = jnp.zeros((1, tm), F32)
    for g in range(N_GROUPS):
        mine = g_idx == g
        cnt = jnp.sum(mine.astype(F32), axis=1, keepdims=True)[0, 0].astype(jnp.int32)
        pcs.append(((cnt + (ROW_PACK - 1)) // ROW_PACK) * ROW_PACK)
        offs.append(run)
        pos = pos + jnp.where(mine, rank[g:g + 1, :] + jnp.asarray(run, F32), 0.0)
        run = run + pcs[g]
    perm = (lax.broadcasted_iota(jnp.int32, (SORT_ROWS, tm), 0) == pos.astype(jnp.int32)).astype(BF16)
    cw = jnp.where(sub == i1, w1, 0.0) + jnp.where(sub == i2, w2, 0.0)
    cw_hi = cw.astype(BF16).astype(F32)
    packed = cw_hi + pltpu.roll(cw - cw_hi, EXPERTS_PER_GROUP, 0)
    cw_cols = _rows_to_cols(jnp.concatenate([packed, jnp.zeros((LANES - ROW_PACK, tm), F32)], axis=0))
    sorted_rows = _dot(perm, jnp.concatenate([hb, cw_cols.astype(BF16)], axis=1)).astype(BF16)
    yield sorted_rows, pos, pcs, offs


def _mixer_call(x2, vecs1, vecs2, w2p, bgk, gn, fill, wrt, brt, weights, experts_f32, cap, bsz, seq):
    g1, sc1, sh1, gt1 = vecs1
    g2, sc2, sh2 = vecs2
    ns = seq // TM_MIX
    n_steps = bsz * ns
    tok = pl.BlockSpec((TM_MIX, D_MODEL), lambda b, s: (b * ns + s, 0))
    vec = pl.BlockSpec((None, 1, D_MODEL), lambda b, s: (b, 0, 0))
    full = lambda a: pl.BlockSpec(a.shape, lambda b, s: (0,) * a.ndim)
    hbm = pl.BlockSpec(memory_space=pl.ANY)
    slab = lambda a: pl.BlockSpec((a.shape[0] // n_steps, a.shape[1]), lambda b, s: (b * ns + s, 0))
    idx = jnp.arange(MIX_SUB)
    tri = ((idx[:, None] // GLA_CHUNK == idx[None, :] // GLA_CHUNK) & (idx[None, :] <= idx[:, None])).astype(BF16)
    upper = (idx[:, None] < idx[None, :]).astype(BF16)
    ex2d = [e.reshape(-1, e.shape[-1]) for e in experts_f32]
    n_sub = n_steps * SUBS
    vmem_in = [x2, g1, sc1, sh1, gt1, w2p, bgk, tri, gn, fill, g2, sc2, sh2, wrt, brt, upper]
    outs = pl.pallas_call(
        _mixer_kernel,
        grid=(bsz, ns),
        in_specs=[tok, full(g1), vec, vec, vec, full(w2p), full(bgk), full(tri), full(gn), full(fill),
                  full(g2), vec, vec, full(wrt), full(brt), full(upper)]
        + [slab(e) for e in ex2d] + [hbm] * N_WEIGHTS,
        out_specs=[tok, hbm, hbm, pl.BlockSpec(memory_space=pltpu.SMEM)] + [slab(e) for e in ex2d],
        out_shape=[jax.ShapeDtypeStruct(x2.shape, F32),
                   jax.ShapeDtypeStruct((N_GROUPS, cap, HS_W), BF16),
                   jax.ShapeDtypeStruct((n_sub, SUBLANES, MIX_SUB), F32),
                   jax.ShapeDtypeStruct((n_sub * META,), jnp.int32)]
        + [jax.ShapeDtypeStruct(e.shape, BF16) for e in ex2d],
        scratch_shapes=[pltpu.VMEM(wt.shape, BF16) for wt in weights]
        + [pltpu.VMEM((GLA_HEADS, GLA_DK, GLA_DV), F32),
           pltpu.VMEM((WINDOW, SWA_KV_W), BF16), pltpu.VMEM((WINDOW, SWA_KV_W), BF16),
           pltpu.VMEM((MIX_SUB, GLA_VAL_W), F32), pltpu.VMEM((MIX_SUB, SWA_W), F32),
           pltpu.VMEM((SUBS, SORT_ROWS, HS_W), BF16), pltpu.VMEM((SUBS, SUBLANES, MIX_SUB), F32),
           pltpu.VMEM((MIX_SUB, D_MODEL), BF16), pltpu.VMEM((TM_EXP, HS_W), BF16),
           pltpu.SMEM((N_GROUPS,), jnp.int32), pltpu.SemaphoreType.DMA((SUBS,))],
        compiler_params=_params(("arbitrary", "arbitrary")),
        name="mixer",
    )(*vmem_in, *ex2d, *weights)
    x1, hs, pos, meta = outs[:4]
    experts_bf16 = [o.reshape(e.shape) for o, e in zip(outs[4:], experts_f32)]
    return x1, hs, pos, meta, experts_bf16


def _experts_kernel(grp_ref, blk_ref, valid_ref, hs_ref, wg_ref, wu_ref, wd_ref, ys_ref):
    @pl.when(valid_ref[pl.program_id(0)] != 0)
    def _():
        hx = hs_ref[...]
        h = hx[:, :D_MODEL]
        cw = hx[:, D_MODEL:].astype(F32)
        gates = [_dot(h, wg_ref[j]) for j in range(EXPERTS_PER_GROUP)]
        ups = [_dot(h, wu_ref[j]) for j in range(EXPERTS_PER_GROUP)]
        hids = [(_silu(gates[j]) * ups[j]
                 * (cw[:, j:j + 1] + cw[:, EXPERTS_PER_GROUP + j:EXPERTS_PER_GROUP + j + 1])).astype(BF16)
                for j in range(EXPERTS_PER_GROUP)]
        y = _dot(hids[0], wd_ref[0])
        for j in range(1, EXPERTS_PER_GROUP):
            y = y + _dot(hids[j], wd_ref[j])
        ys_ref[...] = y.astype(BF16)


def _experts_call(grp, blk, valid, hs, wg, wu, wd):
    n_work = grp.shape[0]
    rows = lambda width: pl.BlockSpec((None, TM_EXP, width), lambda i, grp, blk, valid: (grp[i], blk[i], 0))
    wspec = lambda a: pl.BlockSpec((EXPERTS_PER_GROUP,) + a.shape[1:], lambda i, grp, blk, valid: (grp[i], 0, 0))
    return pl.pallas_call(
        _experts_kernel,
        grid_spec=pltpu.PrefetchScalarGridSpec(
            num_scalar_prefetch=3,
            grid=(n_work,),
            in_specs=[rows(HS_W), wspec(wg), wspec(wu), wspec(wd)],
            out_specs=rows(D_MODEL)),
        out_shape=jax.ShapeDtypeStruct(hs.shape[:2] + (D_MODEL,), BF16),
        compiler_params=_params(("arbitrary",)),
        name="experts",
    )(grp, blk, valid, hs, wg, wu, wd)


def _final_kernel(meta_ref, x1_ref, pos_ref, gt_ref, gf_ref, ys_hbm, o_ref, ybuf_ref, sem):
    step = pl.program_id(0)

    def fetch(s, action):
        half = s % 2
        for sub in range(FIN_SUBS):
            m0 = (s * FIN_SUBS + sub) * META
            bases = [meta_ref[m0 + g] for g in range(N_GROUPS)]
            pcs = [meta_ref[m0 + N_GROUPS + g] for g in range(N_GROUPS)]
            offs, run = [], 0
            for g in range(N_GROUPS):
                offs.append(run)
                run = run + pcs[g]
            _segment_copies(pcs, bases, offs, lambda g, src_row, dst_row, size, sub=sub: pltpu.make_async_copy(
                ys_hbm.at[g, pl.ds(src_row, size), :], ybuf_ref.at[half, sub, pl.ds(dst_row, size), :],
                sem.at[half]), action)

    @pl.when(step == 0)
    def _():
        ybuf_ref[...] = jnp.zeros_like(ybuf_ref)
        fetch(step, "start")

    @pl.when(step + 1 < pl.num_programs(0))
    def _():
        fetch(step + 1, "start")

    fetch(step, "wait")
    half = step % 2
    for sub in range(FIN_SUBS):
        r0 = sub * MIX_SUB
        pos_col = _rows_to_cols(pos_ref[sub])[:, 0:1].astype(jnp.int32)
        sel = (lax.broadcasted_iota(jnp.int32, (MIX_SUB, SORT_ROWS), 1) == pos_col).astype(BF16)
        y = _dot(sel, ybuf_ref[half, sub])
        o_ref[r0:r0 + MIX_SUB, :] = _rms(x1_ref[r0:r0 + MIX_SUB, :] + gt_ref[...] * y, gf_ref[...])


def _final_call(meta, x1, pos, gt2, gf, ys, seq):
    t = x1.shape[0]
    per_b = seq // TM_FIN
    tok = pl.BlockSpec((TM_FIN, D_MODEL), lambda i, meta: (i, 0))
    return pl.pallas_call(
        _final_kernel,
        grid_spec=pltpu.PrefetchScalarGridSpec(
            num_scalar_prefetch=1,
            grid=(t // TM_FIN,),
            in_specs=[tok, pl.BlockSpec((FIN_SUBS, SUBLANES, MIX_SUB), lambda i, meta: (i, 0, 0)),
                      pl.BlockSpec((None, 1, D_MODEL), lambda i, meta: (i // per_b, 0, 0)),
                      pl.BlockSpec((1, D_MODEL), lambda i, meta: (0, 0)),
                      pl.BlockSpec(memory_space=pl.ANY)],
            out_specs=tok,
            scratch_shapes=[pltpu.VMEM((2, FIN_SUBS, SORT_ROWS, D_MODEL), BF16), pltpu.SemaphoreType.DMA((2,))]),
        out_shape=jax.ShapeDtypeStruct((t, D_MODEL), F32),
        compiler_params=_params(("arbitrary",)),
        name="final",
    )(meta, x1, pos, gt2, gf, ys)


def kernel(x, c, w_ada, b_ada, norm1_g, w_in, w_gk2, b_gk, gla_norm_g, sink, w_o, norm2_g,
           w_group, b_group, w_router, b_router, w_gate, w_up, w_down, norm_f_g):
    bsz, seq, d = x.shape
    assert w_ada.shape[0] == 1, "single layer: the final norm is fused behind the layer's MoE"
    l = 0
    t = bsz * seq
    n_sub = t // MIX_SUB
    cap = -(-(t + n_sub * (ROW_PACK - 1) + TM_EXP) // TM_EXP) * TM_EXP
    n_work = t // TM_EXP + n_sub * (ROW_PACK - 1) * N_GROUPS // TM_EXP + N_GROUPS + 1

    mod = _mod_call(c, w_ada[l], b_ada[l])
    sh1, sc1, gt1, sh2, sc2, gt2 = [m.reshape(bsz, 1, d) for m in jnp.split(mod, N_MOD, axis=-1)]

    w_parts = _split_cols(w_in[l])
    w_parts[GATE] = jnp.pad(w_parts[GATE], ((0, 0), (0, LANES - GLA_GATE_RANK)))
    weights = [wt.astype(BF16) for wt in w_parts + [w_o[l]]]
    w2p = jnp.pad(w_gk2[l], ((0, LANES - GLA_GATE_RANK), (0, 0))).astype(BF16)
    fill = jnp.full((SWA_HEADS, 2 * WINDOW), -jnp.inf, F32).at[:, 0].set(sink[l])
    n_logit = N_GROUPS + N_EXPERTS
    wrt = jnp.zeros((n_logit, ROUTE_STRIDE, d), F32).at[:, 0, :].set(
        jnp.concatenate([w_group[l], w_router[l]], axis=1).T).reshape(n_logit * ROUTE_STRIDE, d).astype(BF16)
    brt = jnp.zeros((n_logit, ROUTE_STRIDE), F32).at[:, 0].set(
        jnp.concatenate([b_group[l], b_router[l]])).reshape(n_logit * ROUTE_STRIDE, 1)

    x1, hs, pos, meta, (wg, wu, wd) = _mixer_call(
        x.reshape(t, d), (norm1_g[l].reshape(1, d), sc1, sh1, gt1), (norm2_g[l].reshape(1, d), sc2, sh2),
        w2p, b_gk[l].reshape(1, GLA_KEY_W), gla_norm_g[l].reshape(1, GLA_DV), fill, wrt, brt, weights,
        (w_gate[l], w_up[l], w_down[l]), cap, bsz, seq)

    last = meta.reshape(n_sub, META)[-1]
    totals = last[:N_GROUPS] + last[N_GROUPS:]
    ends = jnp.cumsum((totals + TM_EXP - 1) // TM_EXP)
    item = jnp.minimum(jnp.arange(n_work, dtype=jnp.int32), ends[-1] - 1)
    grp = jnp.sum(item[:, None] >= ends[None, :], axis=1).astype(jnp.int32)
    blk = item - jnp.concatenate([jnp.zeros((1,), ends.dtype), ends[:-1]])[grp]
    valid = (jnp.arange(n_work) < ends[-1]).astype(jnp.int32)

    ys = _experts_call(grp, blk.astype(jnp.int32), valid, hs, wg, wu, wd)
    out = _final_call(meta, x1, pos, gt2, norm_f_g.reshape(1, d), ys, seq)
    return out.reshape(bsz, seq, d)


def _split_cols(w):
    parts, start = [], 0
    for width in IN_WIDTHS:
        parts.append(w[:, start:start + width])
        start += width
    return parts
```

```python
import functools

import jax
import jax.numpy as jnp
from jax import lax
from jax.experimental import pallas as pl
from jax.experimental.pallas import tpu as pltpu

F32 = jnp.float32
BF16 = jnp.bfloat16

D_MODEL = 1024
GLA_HEADS = 4
GLA_DK = 128
GLA_DV = 256
GLA_KEY_W = GLA_HEADS * GLA_DK
GLA_VAL_W = GLA_HEADS * GLA_DV
GLA_GATE_RANK = 16
GLA_TAU = 16.0
GLA_CHUNK = 64
SWA_HEADS = 16
SWA_KV_HEADS = 2
SWA_GROUP = SWA_HEADS // SWA_KV_HEADS
SWA_DH = 64
SWA_W = SWA_HEADS * SWA_DH
SWA_KV_W = SWA_KV_HEADS * SWA_DH
WINDOW = 128
N_GROUPS = 4
EXPERTS_PER_GROUP = 4
N_EXPERTS = N_GROUPS * EXPERTS_PER_GROUP
D_EXPERT = 256
N_MOD = 6
EPS = 1e-6
IN_WIDTHS = (GLA_KEY_W, GLA_KEY_W, GLA_VAL_W, GLA_VAL_W, GLA_GATE_RANK,
             SWA_W, SWA_KV_W, SWA_KV_W, D_MODEL, D_MODEL)

LANES = 128
LOG2_E = 1.4426950408889634
LN_2 = 0.6931471805599453
VMEM_LIMIT = 56 * 1024 * 1024

TM_MIX = 512
MIX_SUB = 256
SUBS = TM_MIX // MIX_SUB
TM_FIN = 2048
FIN_SUBS = TM_FIN // MIX_SUB
ROW_PACK = 16
SORT_ROWS = MIX_SUB + 64
SEG_BITS = (256, 128, 64, 32, 16)
HS_W = D_MODEL + LANES
TM_EXP = 1024
META = 2 * N_GROUPS
SUBLANES = 8
ROUTE_STRIDE = SUBLANES
TN_MOD = 1024


def _params(sem):
    return pltpu.CompilerParams(dimension_semantics=sem, vmem_limit_bytes=VMEM_LIMIT)


def _split(a):
    hi = a.astype(BF16)
    lo = (a - hi.astype(F32)).astype(BF16)
    return hi, lo


def _dot(a, b):
    return jnp.dot(a, b, preferred_element_type=F32)


def _dot_nt(a, b):
    return lax.dot_general(a, b, (((1,), (1,)), ((), ())), preferred_element_type=F32)


def _dot_tn(a, b):
    return lax.dot_general(a, b, (((0,), (0,)), ((), ())), preferred_element_type=F32)


def _dot3(a, b):
    a_hi, a_lo = _split(a)
    b_hi, b_lo = _split(b)
    return _dot(a_hi, b_hi) + _dot(a_hi, b_lo) + _dot(a_lo, b_hi)


def _sigmoid(x):
    return 1.0 / (1.0 + jnp.exp(-x))


def _silu(x):
    return x * _sigmoid(x)


def _rms(x, g):
    return x * lax.rsqrt(jnp.mean(x * x, axis=-1, keepdims=True) + EPS) * g


def _first_of(vals, target):
    idx = jnp.full(target.shape, len(vals) - 1, jnp.int32)
    for j in range(len(vals) - 2, -1, -1):
        idx = jnp.where(vals[j] == target, j, idx)
    return idx


def _rows_to_cols(a):
    return jnp.concatenate([jnp.transpose(a[:, i * LANES:(i + 1) * LANES]) for i in range(a.shape[1] // LANES)],
                           axis=0)


def _segment_copies(pcs, src_rows, dst_rows, make_copy, action):
    for g in range(N_GROUPS):
        for bit in SEG_BITS:
            done = pcs[g] & ~(2 * bit - 1)

            @pl.when((pcs[g] & bit) != 0)
            def _(g=g, bit=bit, done=done):
                cp = make_copy(g, pl.multiple_of(src_rows[g] + done, ROW_PACK),
                               pl.multiple_of(dst_rows[g] + done, ROW_PACK), bit)
                cp.start() if action == "start" else cp.wait()


def _mod_kernel(c_ref, w_ref, b_ref, o_ref):
    o_ref[...] = _dot3(_silu(c_ref[...]), w_ref[...]) + b_ref[...]


def _mod_call(c, w_ada, b_ada):
    bsz = c.shape[0]
    n = w_ada.shape[1]
    return pl.pallas_call(
        _mod_kernel,
        grid=(n // TN_MOD,),
        in_specs=[pl.BlockSpec((bsz, D_MODEL), lambda j: (0, 0)),
                  pl.BlockSpec((D_MODEL, TN_MOD), lambda j: (0, j)),
                  pl.BlockSpec((1, TN_MOD), lambda j: (0, j))],
        out_specs=pl.BlockSpec((bsz, TN_MOD), lambda j: (0, j)),
        out_shape=jax.ShapeDtypeStruct((bsz, n), F32),
        compiler_params=_params(("parallel",)),
        name="mod",
    )(c, w_ada, b_ada.reshape(1, n))


QA, KA, VA, RA, GATE, QB, KB, VB, GA, GB, WO = range(11)
N_WEIGHTS = 11


def _mixer_kernel(x_ref, g1_ref, sc1_ref, sh1_ref, gt1_ref, w2_ref, bgk_ref, tri_ref, gn_ref, fill_ref,
                  g2_ref, sc2_ref, sh2_ref, wrt_ref, brt_ref, upper_ref, wgf_ref, wuf_ref, wdf_ref, *rest):
    w_hbm = rest[:N_WEIGHTS]
    x1_ref, hs_hbm, pos_hbm, meta_ref, wgb_ref, wub_ref, wdb_ref = rest[N_WEIGHTS:N_WEIGHTS + 7]
    w = rest[N_WEIGHTS + 7:2 * N_WEIGHTS + 7]
    (st_ref, kprev_ref, vprev_ref, oa_ref, ob_ref, stage_ref, pstage_ref, hkeep_ref, zero_ref, base_ref,
     sem) = rest[2 * N_WEIGHTS + 7:]
    first = pl.program_id(1) == 0
    step = pl.program_id(0) * pl.num_programs(1) + pl.program_id(1)
    n_steps = pl.num_programs(0) * pl.num_programs(1)

    @pl.when(step == 0)
    def _():
        loads = [pltpu.make_async_copy(src, dst, sem.at[0]) for src, dst in zip(w_hbm, w)]
        for cp in loads:
            cp.start()
        for cp in loads:
            cp.wait()
        for g in range(N_GROUPS):
            base_ref[g] = 0
        hkeep_ref[...] = jnp.zeros_like(hkeep_ref)

    @pl.when(first)
    def _():
        st_ref[...] = jnp.zeros_like(st_ref)
        kprev_ref[...] = jnp.zeros_like(kprev_ref)
        vprev_ref[...] = jnp.zeros_like(vprev_ref)

    wgb_ref[...] = wgf_ref[...].astype(BF16)
    wub_ref[...] = wuf_ref[...].astype(BF16)
    wdb_ref[...] = wdf_ref[...].astype(BF16)

    def stage_copy(slot):
        return lambda g, src_row, dst_row, size: pltpu.make_async_copy(
            stage_ref.at[slot, pl.ds(src_row, size), :], hs_hbm.at[g, pl.ds(dst_row, size), :], sem.at[slot])

    pos_copy = lambda slot, tile: pltpu.make_async_copy(pstage_ref.at[slot], pos_hbm.at[tile], sem.at[slot])
    zeros4 = [0] * N_GROUPS

    def wait_copies(sub, tile, pcs):
        _segment_copies(pcs, zeros4, zeros4, stage_copy(sub), "wait")
        pos_copy(sub, tile).wait()

    def publish(sub, owner, routed):
        sorted_rows, pos, pcs, offs = routed
        tile = owner * SUBS + sub
        m0 = tile * META

        @pl.when(owner > 0)
        def _():
            wait_copies(sub, tile, [meta_ref[m0 - SUBS * META + N_GROUPS + g] for g in range(N_GROUPS)])

        stage_ref[sub] = sorted_rows
        pstage_ref[sub] = jnp.broadcast_to(pos, (SUBLANES, MIX_SUB))
        bases = [base_ref[g] for g in range(N_GROUPS)]
        for g in range(N_GROUPS):
            meta_ref[m0 + g] = bases[g]
            meta_ref[m0 + N_GROUPS + g] = pcs[g]
            base_ref[g] = bases[g] + pcs[g]
        _segment_copies(pcs, offs, bases, stage_copy(sub), "start")
        pos_copy(sub, tile).start()

    route = lambda hb: _route_and_sort(hb, wrt_ref, brt_ref, upper_ref)
    moe_in = lambda x1: (_rms(x1, g2_ref[...]) * (1.0 + sc2_ref[...]) + sh2_ref[...]).astype(BF16)
    last_step = step == n_steps - 1

    routing, results = route(hkeep_ref[...]), []
    for sub in range(SUBS):
        r0 = sub * MIX_SUB
        x1, routed = _mixer_tile(x_ref[r0:r0 + MIX_SUB, :], first if sub == 0 else False, g1_ref, sc1_ref,
                                 sh1_ref, gt1_ref, w2_ref, bgk_ref, tri_ref, gn_ref, fill_ref, w, st_ref,
                                 kprev_ref, vprev_ref, oa_ref, ob_ref, side=routing)
        x1_ref[r0:r0 + MIX_SUB, :] = x1
        results.append(routed)
        if sub + 1 < SUBS:
            routing = route(moe_in(x1))
        else:
            hkeep_ref[...] = moe_in(x1)

    @pl.when(step > 0)
    def _():
        publish(SUBS - 1, step - 1, results[0])

    for sub in range(1, SUBS):
        publish(sub - 1, step, results[sub])

    @pl.when(last_step)
    def _():
        routing = route(hkeep_ref[...])
        next(routing)
        next(routing)
        routed = next(routing)
        publish(SUBS - 1, step, routed)
        for sub in range(1, SUBS):
            wait_copies(sub - 1, step * SUBS + sub - 1, results[sub][2])
        wait_copies(SUBS - 1, step * SUBS + SUBS - 1, routed[2])
        zero_ref[...] = jnp.zeros_like(zero_ref)
        tails = [pltpu.make_async_copy(zero_ref, hs_hbm.at[g, pl.ds(pl.multiple_of(base_ref[g], ROW_PACK), TM_EXP), :],
                                       sem.at[0]) for g in range(N_GROUPS)]
        for cp in tails:
            cp.start()
        for cp in tails:
            cp.wait()


def _mixer_tile(x, first, g1_ref, sc_ref, sh_ref, gt_ref, w2_ref, bgk_ref, tri_ref, gn_ref, fill_ref,
                w, st_ref, kprev_ref, vprev_ref, oa_ref, ob_ref, side=None):
    ts = MIX_SUB
    advance = (lambda: next(side)) if side is not None else (lambda: None)
    hb = (_rms(x, g1_ref[...]) * (1.0 + sc_ref[...]) + sh_ref[...]).astype(BF16)

    z = _dot(hb, w[GATE][...])
    gz = _dot(z.astype(BF16), w2_ref[...]) + bgk_ref[...]
    soft = jnp.log2(1.0 + jnp.exp2(jnp.abs(gz) * (-LOG2_E)))
    la_hi, la_lo = _split(jnp.minimum(gz, 0.0) * (1.0 / GLA_TAU) - soft * (LN_2 / GLA_TAU))
    qa = _dot(hb, w[QA][...])
    ka = _dot(hb, w[KA][...])
    advance()
    va = _dot(hb, w[VA][...]).astype(BF16)
    bc = _dot(tri_ref[...], la_hi) + _dot(tri_ref[...], la_lo)
    ra = _dot(hb, w[RA][...])
    advance()
    qb = _dot(hb, w[QB][...]).astype(BF16) * jnp.asarray(SWA_DH ** -0.5, BF16)
    kb = _dot(hb, w[KB][...]).astype(BF16)
    vb = _dot(hb, w[VB][...]).astype(BF16)

    nc = ts // GLA_CHUNK
    chunk = lambda c: slice(c * GLA_CHUNK, (c + 1) * GLA_CHUNK)
    btot = [bc[(c + 1) * GLA_CHUNK - 1:(c + 1) * GLA_CHUNK, :] for c in range(nc)]
    pre = [jnp.zeros_like(btot[0])]
    for c in range(nc):
        pre.append(pre[c] + btot[c])
    rows = lambda vecs: jnp.concatenate([jnp.broadcast_to(v, (GLA_CHUNK, GLA_KEY_W)) for v in vecs], axis=0)
    qf = qa * (GLA_DK ** -0.5) * jnp.exp(bc)
    q_dec = qf.astype(BF16)
    q_in = (qf * rows([jnp.exp(pre[c]) for c in range(nc)])).astype(BF16)
    k_inv = (ka * jnp.exp(-bc)).astype(BF16)
    k_end = ka * jnp.exp(rows(btot) - bc)
    k_tile = (k_end * rows([jnp.exp(pre[nc] - pre[c + 1]) for c in range(nc)])).astype(BF16)
    k_cross = {(c, c2): (k_end[chunk(c2)] * jnp.exp(pre[c] - pre[c2 + 1])).astype(BF16)
               for c in range(nc) for c2 in range(c)}
    dec_tile = jnp.exp(pre[nc])
    causal = (lax.broadcasted_iota(jnp.int32, (ts, ts), 1) <= lax.broadcasted_iota(jnp.int32, (ts, ts), 0))

    def gla_scores(h):
        ks = slice(h * GLA_DK, (h + 1) * GLA_DK)
        att = []
        for c in range(nc):
            keys = [k_cross[(c, c2)][:, ks] for c2 in range(c)] + [k_inv[chunk(c), ks]]
            keys += [jnp.zeros((GLA_CHUNK, GLA_DK), BF16)] * (nc - 1 - c)
            att.append(_dot_nt(q_dec[chunk(c), ks], jnp.concatenate(keys, axis=0)))
        return jnp.where(causal, jnp.concatenate(att, axis=0), 0.0).astype(BF16)

    atts = [gla_scores(h) for h in range(GLA_HEADS)]
    gn = gn_ref[...]
    for h in range(GLA_HEADS):
        ks = slice(h * GLA_DK, (h + 1) * GLA_DK)
        vs = slice(h * GLA_DV, (h + 1) * GLA_DV)
        st = st_ref[h]
        o = _dot(atts[h], va[:, vs]) + _dot(q_in[:, ks], st.astype(BF16))
        dec_col = jnp.transpose(jnp.broadcast_to(dec_tile[:, ks], (SUBLANES, GLA_DK)))[:, 0:1]
        st_ref[h] = st * dec_col + _dot_tn(k_tile[:, ks], va[:, vs])
        o = o * lax.rsqrt(jnp.mean(o * o, axis=-1, keepdims=True) + EPS) * gn
        oa_ref[:, vs] = o * _silu(ra[:, vs])

    wd = WINDOW
    k_all = jnp.concatenate([kprev_ref[...], kb], axis=0)
    v_all = jnp.concatenate([vprev_ref[...], vb], axis=0)
    kprev_ref[...] = kb[ts - wd:, :]
    vprev_ref[...] = vb[ts - wd:, :]
    lo_half = lax.broadcasted_iota(jnp.int32, (wd + ts, LANES), 1) < SWA_DH
    k_sw = pltpu.roll(k_all, SWA_DH, 1)
    v_sw = pltpu.roll(v_all, SWA_DH, 1)
    k2 = (jnp.where(lo_half, k_all, k_sw), jnp.where(lo_half, k_sw, k_all))
    v2 = (jnp.where(lo_half, v_all, v_sw), jnp.where(lo_half, v_sw, v_all))
    qi = lax.broadcasted_iota(jnp.int32, (wd, 2 * wd), 0) + wd
    kj = lax.broadcasted_iota(jnp.int32, (wd, 2 * wd), 1)
    band = (qi - kj >= 0) & (qi - kj < wd)
    band_first = band & (jnp.logical_not(first) | (kj >= wd)) if first is not False else band
    q_lo = lax.broadcasted_iota(jnp.int32, (wd, LANES), 1) < SWA_DH
    row0 = lax.broadcasted_iota(jnp.int32, (2 * wd, LANES), 0) == 0
    ones = jnp.ones((2 * wd, LANES), BF16)
    half = SWA_GROUP // 2
    combos = [(blk, hk) for blk in range(ts // wd) for hk in range(SWA_KV_HEADS)]

    def swa_probs(blk, hk):
        parts = []
        for p in range(half):
            qp = qb[blk * wd:(blk + 1) * wd, (hk * half + p) * LANES:(hk * half + p + 1) * LANES]
            parts += [jnp.where(q_lo, qp, jnp.zeros_like(qp)), jnp.where(q_lo, jnp.zeros_like(qp), qp)]
        s = _dot_nt(jnp.concatenate(parts, axis=0), k2[hk][blk * wd:blk * wd + 2 * wd])
        valid = band_first if blk == 0 else band
        es = []
        for r in range(SWA_GROUP):
            head = hk * SWA_GROUP + r
            sh = jnp.where(valid, s[r * wd:(r + 1) * wd], fill_ref[head:head + 1, :])
            es.append(jnp.exp(sh - jnp.max(sh, axis=-1, keepdims=True)).astype(BF16))
        return jnp.concatenate(es, axis=0)

    probs = [swa_probs(blk, hk) for blk, hk in combos]
    side_out = advance()
    ga = _dot(hb, w[GA][...])
    gb = _dot(hb, w[GB][...])
    for (blk, hk), e in zip(combos, probs):
        vv = jnp.where(row0, jnp.zeros_like(ones), v2[hk][blk * wd:blk * wd + 2 * wd])
        o = _dot(e, jnp.concatenate([vv, ones], axis=1))
        for p in range(half):
            oe = o[(2 * p) * wd:(2 * p + 1) * wd]
            oo = o[(2 * p + 1) * wd:(2 * p + 2) * wd]
            num = jnp.where(q_lo, oe[:, :LANES], oo[:, :LANES])
            den = jnp.where(q_lo, oe[:, LANES:], oo[:, LANES:])
            ob_ref[blk * wd:(blk + 1) * wd, (hk * half + p) * LANES:(hk * half + p + 1) * LANES] = num / den

    merged = _sigmoid(ga) * oa_ref[...] + _sigmoid(gb) * ob_ref[...]
    return x + gt_ref[...] * _dot(merged.astype(BF16), w[WO][...]), side_out


def _route_and_sort(hb, wrt_ref, brt_ref, upper_ref):
    tm = hb.shape[0]
    neg = -jnp.inf
    lt = _dot_nt(wrt_ref[...], hb) + brt_ref[...]
    yield None
    lrow = lambda k: lt[ROUTE_STRIDE * k:ROUTE_STRIDE * k + 1, :]
    lg = [lrow(g) for g in range(N_GROUPS)]
    gmax = functools.reduce(jnp.maximum, lg)
    g_w = 1.0 / functools.reduce(jnp.add, [jnp.exp(v - gmax) for v in lg])
    g_idx = _first_of(lg, gmax)
    le = []
    for j in range(EXPERTS_PER_GROUP):
        v = lrow(N_GROUPS + (N_GROUPS - 1) * EXPERTS_PER_GROUP + j)
        for g in range(N_GROUPS - 2, -1, -1):
            v = jnp.where(g_idx == g, lrow(N_GROUPS + g * EXPERTS_PER_GROUP + j), v)
        le.append(v)
    v1 = functools.reduce(jnp.maximum, le)
    i1 = _first_of(le, v1)
    le2 = [jnp.where(i1 == j, neg, le[j]) for j in range(EXPERTS_PER_GROUP)]
    v2 = functools.reduce(jnp.maximum, le2)
    i2 = _first_of(le2, v2)
    e2 = jnp.exp(v2 - v1)
    w1 = g_w / (1.0 + e2)
    w2 = g_w * e2 / (1.0 + e2)

    sub = lax.broadcasted_iota(jnp.int32, (ROW_PACK, tm), 0)
    rank = _dot((sub == g_idx).astype(BF16), upper_ref[...])
    yield None
    pcs, offs, run = [], [], 0
    pos = jnp.zeros((1, tm), F32)
    for g in range(N_GROUPS):
        mine = g_idx == g
        cnt = jnp.sum(mine.astype(F32), axis=1, keepdims=True)[0, 0].astype(jnp.int32)
        pcs.append(((cnt + (ROW_PACK - 1)) // ROW_PACK) * ROW_PACK)
        offs.append(run)
        pos = pos + jnp.where(mine, rank[g:g + 1, :] + jnp.asarray(run, F32), 0.0)
        run = run + pcs[g]
    perm = (lax.broadcasted_iota(jnp.int32, (SORT_ROWS, tm), 0) == pos.astype(jnp.int32)).astype(BF16)
    cw = jnp.where(sub == i1, w1, 0.0) + jnp.where(sub == i2, w2, 0.0)
    cw_hi = cw.astype(BF16).astype(F32)
    packed = cw_hi + pltpu.roll(cw - cw_hi, EXPERTS_PER_GROUP, 0)
    cw_cols = _rows_to_cols(jnp.concatenate([packed, jnp.zeros((LANES - ROW_PACK, tm), F32)], axis=0))
    sorted_rows = _dot(perm, jnp.concatenate([hb, cw_cols.astype(BF16)], axis=1)).astype(BF16)
    yield sorted_rows, pos, pcs, offs


def _mixer_call(x2, vecs1, vecs2, w2p, bgk, gn, fill, wrt, brt, weights, experts_f32, cap, bsz, seq):
    g1, sc1, sh1, gt1 = vecs1
    g2, sc2, sh2 = vecs2
    ns = seq // TM_MIX
    n_steps = bsz * ns
    tok = pl.BlockSpec((TM_MIX, D_MODEL), lambda b, s: (b * ns + s, 0))
    vec = pl.BlockSpec((None, 1, D_MODEL), lambda b, s: (b, 0, 0))
    full = lambda a: pl.BlockSpec(a.shape, lambda b, s: (0,) * a.ndim)
    hbm = pl.BlockSpec(memory_space=pl.ANY)
    slab = lambda a: pl.BlockSpec((a.shape[0] // n_steps, a.shape[1]), lambda b, s: (b * ns + s, 0))
    idx = jnp.arange(MIX_SUB)
    tri = ((idx[:, None] // GLA_CHUNK == idx[None, :] // GLA_CHUNK) & (idx[None, :] <= idx[:, None])).astype(BF16)
    upper = (idx[:, None] < idx[None, :]).astype(BF16)
    ex2d = [e.reshape(-1, e.shape[-1]) for e in experts_f32]
    n_sub = n_steps * SUBS
    vmem_in = [x2, g1, sc1, sh1, gt1, w2p, bgk, tri, gn, fill, g2, sc2, sh2, wrt, brt, upper]
    outs = pl.pallas_call(
        _mixer_kernel,
        grid=(bsz, ns),
        in_specs=[tok, full(g1), vec, vec, vec, full(w2p), full(bgk), full(tri), full(gn), full(fill),
                  full(g2), vec, vec, full(wrt), full(brt), full(upper)]
        + [slab(e) for e in ex2d] + [hbm] * N_WEIGHTS,
        out_specs=[tok, hbm, hbm, pl.BlockSpec(memory_space=pltpu.SMEM)] + [slab(e) for e in ex2d],
        out_shape=[jax.ShapeDtypeStruct(x2.shape, F32),
                   jax.ShapeDtypeStruct((N_GROUPS, cap, HS_W), BF16),
                   jax.ShapeDtypeStruct((n_sub, SUBLANES, MIX_SUB), F32),
                   jax.ShapeDtypeStruct((n_sub * META,), jnp.int32)]
        + [jax.ShapeDtypeStruct(e.shape, BF16) for e in ex2d],
        scratch_shapes=[pltpu.VMEM(wt.shape, BF16) for wt in weights]
        + [pltpu.VMEM((GLA_HEADS, GLA_DK, GLA_DV), F32),
           pltpu.VMEM((WINDOW, SWA_KV_W), BF16), pltpu.VMEM((WINDOW, SWA_KV_W), BF16),
           pltpu.VMEM((MIX_SUB, GLA_VAL_W), F32), pltpu.VMEM((MIX_SUB, SWA_W), F32),
           pltpu.VMEM((SUBS, SORT_ROWS, HS_W), BF16), pltpu.VMEM((SUBS, SUBLANES, MIX_SUB), F32),
           pltpu.VMEM((MIX_SUB, D_MODEL), BF16), pltpu.VMEM((TM_EXP, HS_W), BF16),
           pltpu.SMEM((N_GROUPS,), jnp.int32), pltpu.SemaphoreType.DMA((SUBS,))],
        compiler_params=_params(("arbitrary", "arbitrary")),
        name="mixer",
    )(*vmem_in, *ex2d, *weights)
    x1, hs, pos, meta = outs[:4]
    experts_bf16 = [o.reshape(e.shape) for o, e in zip(outs[4:], experts_f32)]
    return x1, hs, pos, meta, experts_bf16


def _experts_kernel(grp_ref, blk_ref, valid_ref, hs_ref, wg_ref, wu_ref, wd_ref, ys_ref):
    @pl.when(valid_ref[pl.program_id(0)] != 0)
    def _():
        hx = hs_ref[...]
        h = hx[:, :D_MODEL]
        cw = hx[:, D_MODEL:].astype(F32)
        gates = [_dot(h, wg_ref[j]) for j in range(EXPERTS_PER_GROUP)]
        ups = [_dot(h, wu_ref[j]) for j in range(EXPERTS_PER_GROUP)]
        hids = [(_silu(gates[j]) * ups[j]
                 * (cw[:, j:j + 1] + cw[:, EXPERTS_PER_GROUP + j:EXPERTS_PER_GROUP + j + 1])).astype(BF16)
                for j in range(EXPERTS_PER_GROUP)]
        y = _dot(hids[0], wd_ref[0])
        for j in range(1, EXPERTS_PER_GROUP):
            y = y + _dot(hids[j], wd_ref[j])
        ys_ref[...] = y.astype(BF16)


def _experts_call(grp, blk, valid, hs, wg, wu, wd):
    n_work = grp.shape[0]
    rows = lambda width: pl.BlockSpec((None, TM_EXP, width), lambda i, grp, blk, valid: (grp[i], blk[i], 0))
    wspec = lambda a: pl.BlockSpec((EXPERTS_PER_GROUP,) + a.shape[1:], lambda i, grp, blk, valid: (grp[i], 0, 0))
    return pl.pallas_call(
        _experts_kernel,
        grid_spec=pltpu.PrefetchScalarGridSpec(
            num_scalar_prefetch=3,
            grid=(n_work,),
            in_specs=[rows(HS_W), wspec(wg), wspec(wu), wspec(wd)],
            out_specs=rows(D_MODEL)),
        out_shape=jax.ShapeDtypeStruct(hs.shape[:2] + (D_MODEL,), BF16),
        compiler_params=_params(("arbitrary",)),
        name="experts",
    )(grp, blk, valid, hs, wg, wu, wd)


def _final_kernel(meta_ref, x1_ref, pos_ref, gt_ref, gf_ref, ys_hbm, o_ref, ybuf_ref, sem):
    step = pl.program_id(0)

    def fetch(s, action):
        half = s % 2
        for sub in range(FIN_SUBS):
            m0 = (s * FIN_SUBS + sub) * META
            bases = [meta_ref[m0 + g] for g in range(N_GROUPS)]
            pcs = [meta_ref[m0 + N_GROUPS + g] for g in range(N_GROUPS)]
            offs, run = [], 0
            for g in range(N_GROUPS):
                offs.append(run)
                run = run + pcs[g]
            _segment_copies(pcs, bases, offs, lambda g, src_row, dst_row, size, sub=sub: pltpu.make_async_copy(
                ys_hbm.at[g, pl.ds(src_row, size), :], ybuf_ref.at[half, sub, pl.ds(dst_row, size), :],
                sem.at[half]), action)

    @pl.when(step == 0)
    def _():
        ybuf_ref[...] = jnp.zeros_like(ybuf_ref)
        fetch(step, "start")

    @pl.when(step + 1 < pl.num_programs(0))
    def _():
        fetch(step + 1, "start")

    fetch(step, "wait")
    half = step % 2
    for sub in range(FIN_SUBS):
        r0 = sub * MIX_SUB
        pos_col = _rows_to_cols(pos_ref[sub])[:, 0:1].astype(jnp.int32)
        sel = (lax.broadcasted_iota(jnp.int32, (MIX_SUB, SORT_ROWS), 1) == pos_col).astype(BF16)
        y = _dot(sel, ybuf_ref[half, sub])
        o_ref[r0:r0 + MIX_SUB, :] = _rms(x1_ref[r0:r0 + MIX_SUB, :] + gt_ref[...] * y, gf_ref[...])


def _final_call(meta, x1, pos, gt2, gf, ys, seq):
    t = x1.shape[0]
    per_b = seq // TM_FIN
    tok = pl.BlockSpec((TM_FIN, D_MODEL), lambda i, meta: (i, 0))
    return pl.pallas_call(
        _final_kernel,
        grid_spec=pltpu.PrefetchScalarGridSpec(
            num_scalar_prefetch=1,
            grid=(t // TM_FIN,),
            in_specs=[tok, pl.BlockSpec((FIN_SUBS, SUBLANES, MIX_SUB), lambda i, meta: (i, 0, 0)),
                      pl.BlockSpec((None, 1, D_MODEL), lambda i, meta: (i // per_b, 0, 0)),
                      pl.BlockSpec((1, D_MODEL), lambda i, meta: (0, 0)),
                      pl.BlockSpec(memory_space=pl.ANY)],
            out_specs=tok,
            scratch_shapes=[pltpu.VMEM((2, FIN_SUBS, SORT_ROWS, D_MODEL), BF16), pltpu.SemaphoreType.DMA((2,))]),
        out_shape=jax.ShapeDtypeStruct((t, D_MODEL), F32),
        compiler_params=_params(("arbitrary",)),
        name="final",
    )(meta, x1, pos, gt2, gf, ys)


def kernel(x, c, w_ada, b_ada, norm1_g, w_in, w_gk2, b_gk, gla_norm_g, sink, w_o, norm2_g,
           w_group, b_group, w_router, b_router, w_gate, w_up, w_down, norm_f_g):
    bsz, seq, d = x.shape
    assert w_ada.shape[0] == 1, "single layer: the final norm is fused behind the layer's MoE"
    l = 0
    t = bsz * seq
    n_sub = t // MIX_SUB
    cap = -(-(t + n_sub * (ROW_PACK - 1) + TM_EXP) // TM_EXP) * TM_EXP
    n_work = t // TM_EXP + n_sub * (ROW_PACK - 1) * N_GROUPS // TM_EXP + N_GROUPS + 1

    mod = _mod_call(c, w_ada[l], b_ada[l])
    sh1, sc1, gt1, sh2, sc2, gt2 = [m.reshape(bsz, 1, d) for m in jnp.split(mod, N_MOD, axis=-1)]

    w_parts = _split_cols(w_in[l])
    w_parts[GATE] = jnp.pad(w_parts[GATE], ((0, 0), (0, LANES - GLA_GATE_RANK)))
    weights = [wt.astype(BF16) for wt in w_parts + [w_o[l]]]
    w2p = jnp.pad(w_gk2[l], ((0, LANES - GLA_GATE_RANK), (0, 0))).astype(BF16)
    fill = jnp.full((SWA_HEADS, 2 * WINDOW), -jnp.inf, F32).at[:, 0].set(sink[l])
    n_logit = N_GROUPS + N_EXPERTS
    wrt = jnp.zeros((n_logit, ROUTE_STRIDE, d), F32).at[:, 0, :].set(
        jnp.concatenate([w_group[l], w_router[l]], axis=1).T).reshape(n_logit * ROUTE_STRIDE, d).astype(BF16)
    brt = jnp.zeros((n_logit, ROUTE_STRIDE), F32).at[:, 0].set(
        jnp.concatenate([b_group[l], b_router[l]])).reshape(n_logit * ROUTE_STRIDE, 1)

    x1, hs, pos, meta, (wg, wu, wd) = _mixer_call(
        x.reshape(t, d), (norm1_g[l].reshape(1, d), sc1, sh1, gt1), (norm2_g[l].reshape(1, d), sc2, sh2),
        w2p, b_gk[l].reshape(1, GLA_KEY_W), gla_norm_g[l].reshape(1, GLA_DV), fill, wrt, brt, weights,
        (w_gate[l], w_up[l], w_down[l]), cap, bsz, seq)

    last = meta.reshape(n_sub, META)[-1]
    totals = last[:N_GROUPS] + last[N_GROUPS:]
    ends = jnp.cumsum((totals + TM_EXP - 1) // TM_EXP)
    item = jnp.minimum(jnp.arange(n_work, dtype=jnp.int32), ends[-1] - 1)
    grp = jnp.sum(item[:, None] >= ends[None, :], axis=1).astype(jnp.int32)
    blk = item - jnp.concatenate([jnp.zeros((1,), ends.dtype), ends[:-1]])[grp]
    valid = (jnp.arange(n_work) < ends[-1]).astype(jnp.int32)

    ys = _experts_call(grp, blk.astype(jnp.int32), valid, hs, wg, wu, wd)
    out = _final_call(meta, x1, pos, gt2, norm_f_g.reshape(1, d), ys, seq)
    return out.reshape(bsz, seq, d)


def _split_cols(w):
    parts, start = [], 0
    for width in IN_WIDTHS:
        parts.append(w[:, start:start + width])
        start += width
    return parts
```
